```python
import math
import jax, jax.numpy as jnp
from jax import lax
import numpy as np

D_MODEL = 1024
BATCH = 8
SEQ = 4096
DEPTH = 4

N_MIXERS = 4
RMS_EPS = 1e-6
LN_EPS = 1e-5
NEG_INF = -1e30
BIG = 1e9
D_FF = -(-8 * D_MODEL // (3 * 256)) * 256

CONV_WIDTH = 31

NSA_HEAD_DIM = 64
NSA_HEADS = D_MODEL // NSA_HEAD_DIM
NSA_KV_GROUPS = 4
NSA_CMP_BLOCK = 32
NSA_CMP_STRIDE = 16
NSA_SLC_BLOCK = 64
NSA_TOP_N = 16
NSA_LOCAL_BLOCKS = 2
NSA_WINDOW = 512
NSA_Q_CHUNK = 32
NSA_PROJ = NSA_HEADS * NSA_HEAD_DIM + 6 * NSA_KV_GROUPS * NSA_HEAD_DIM + 3 * NSA_HEADS
ROPE_THETA = 500000.0
ROPE_DIMS = NSA_HEAD_DIM // 4

S5_GROUP = 16
S5_STATE = 64
S5_N_GROUPS = D_MODEL // S5_GROUP

POOL_WINDOWS = (2, 4, 8, 16)
POOL_GROUP = D_MODEL // len(POOL_WINDOWS)

kernel_name = 'hybrid_conv_nsa_s5_pool_trunk'


def _n_layers_of(m):
    return (DEPTH - m + N_MIXERS - 1) // N_MIXERS


def rmsnorm(x, g):
    xf = x.astype(jnp.float32)
    y = xf * lax.rsqrt(jnp.mean(xf * xf, axis=-1, keepdims=True) + RMS_EPS)
    return (y * g.astype(jnp.float32)).astype(x.dtype)


def partial_rope(x, positions):
    half = ROPE_DIMS // 2
    inv_freq = ROPE_THETA ** (-jnp.arange(half, dtype=jnp.float32) / half)
    ang = positions.astype(jnp.float32)[:, None] * inv_freq[None, :]
    cos, sin = jnp.cos(ang), jnp.sin(ang)
    xr = x[..., :ROPE_DIMS].astype(jnp.float32)
    x1, x2 = xr[..., :half], xr[..., half:]
    rot = jnp.concatenate([x1 * cos - x2 * sin, x2 * cos + x1 * sin], axis=-1)
    return jnp.concatenate([rot.astype(x.dtype), x[..., ROPE_DIMS:]], axis=-1)


def masked_softmax(s, mask):
    s = jnp.where(mask, s.astype(jnp.float32), NEG_INF)
    return jax.nn.softmax(s, axis=-1) * mask.astype(jnp.float32)


def swiglu_ffn(h, w_in, w_out):
    gate, up = jnp.split(h @ w_in, 2, axis=-1)
    return (jax.nn.silu(gate) * up) @ w_out


def conformer_conv_module(h, w_in, b_in, w_dw, b_dw, ln_g, ln_b, w_out):
    a, g = jnp.split(h @ w_in + b_in, 2, axis=-1)
    u = a * jax.nn.sigmoid(g)
    u = lax.conv_general_dilated(
        u, w_dw[:, None, :].astype(u.dtype), window_strides=(1,),
        padding=[(CONV_WIDTH - 1, 0)], dimension_numbers=('NWC', 'WIO', 'NWC'),
        feature_group_count=D_MODEL) + b_dw
    uf = u.astype(jnp.float32)
    mu = jnp.mean(uf, axis=-1, keepdims=True)
    var = jnp.mean(jnp.square(uf - mu), axis=-1, keepdims=True)
    uf = (uf - mu) * lax.rsqrt(var + LN_EPS) * ln_g.astype(jnp.float32) + ln_b.astype(jnp.float32)
    u = jax.nn.silu(uf).astype(h.dtype)
    return u @ w_out


def nsa_attention(h, positions, w_in, q_gain, k_gain, cmp_pos, cmp_w1, cmp_b1, cmp_w2, w_out):
    bsz, seq, _ = h.shape
    H, G, dh = NSA_HEADS, NSA_KV_GROUPS, NSA_HEAD_DIM
    R = H // G
    qd, kvd = H * dh, 6 * G * dh
    proj = h @ w_in
    q = proj[..., :qd].reshape(bsz, seq, H, dh).transpose(0, 2, 1, 3)
    kv = proj[..., qd:qd + kvd].reshape(bsz, seq, 3, 2, G, dh).transpose(2, 3, 0, 4, 1, 5)
    gates = jax.nn.sigmoid(proj[..., qd + kvd:].astype(jnp.float32)).reshape(bsz, seq, 3, H)
    q = partial_rope(rmsnorm(q, q_gain), positions)
    k = partial_rope(rmsnorm(kv[:, 0], k_gain[:, None, None, None, :]), positions)
    v = kv[:, 1]

    n_cmp = (seq - NSA_CMP_BLOCK) // NSA_CMP_STRIDE + 1
    blk_idx = np.arange(n_cmp)[:, None] * NSA_CMP_STRIDE + np.arange(NSA_CMP_BLOCK)[None, :]
    kv_c = jnp.stack([k[0], v[0]])
    blocks = kv_c[:, :, :, blk_idx, :] + cmp_pos[:, None, None, None]
    flat = blocks.reshape(2, bsz, G, n_cmp, NSA_CMP_BLOCK * dh)
    hid = jax.nn.gelu(jnp.einsum('cbgnf,cfe->cbgne', flat, cmp_w1) + cmp_b1[:, None, None, None, :])
    comp = jnp.einsum('cbgne,cef->cbgnf', hid, cmp_w2)
    k_cmp, v_cmp = comp[0], comp[1]
    cmp_end = jnp.asarray(blk_idx[:, -1])

    n_sel = seq // NSA_SLC_BLOCK
    top_n = min(NSA_TOP_N, n_sel)
    c_start = np.arange(n_cmp) * NSA_CMP_STRIDE
    c_end = c_start + NSA_CMP_BLOCK - 1
    s_start = np.arange(n_sel) * NSA_SLC_BLOCK
    s_end = s_start + NSA_SLC_BLOCK - 1
    overlap = jnp.asarray(((c_start[:, None] <= s_end[None, :]) &
                           (c_end[:, None] >= s_start[None, :])).astype(np.float32))
    k_slc, v_slc = k[1], v[1]
    pad = ((0, 0), (0, 0), (NSA_WINDOW, 0), (0, 0))
    k_win, v_win = jnp.pad(k[2], pad), jnp.pad(v[2], pad)
    scale = dh ** -0.5
    sel_offsets = jnp.arange(NSA_SLC_BLOCK)
    blk_ids = jnp.arange(n_sel)
    gather = jax.vmap(jax.vmap(lambda arr, idx: arr[idx]))

    def chunk(c):
        q0 = c * NSA_Q_CHUNK
        t = q0 + jnp.arange(NSA_Q_CHUNK)
        qc = lax.dynamic_slice_in_dim(q, q0, NSA_Q_CHUNK, axis=2).reshape(bsz, G, R, NSA_Q_CHUNK, dh)
        s_c = jnp.einsum('bgrqd,bgnd->bgrqn', qc, k_cmp) * scale
        p_c = masked_softmax(s_c, cmp_end[None, :] <= t[:, None])
        o_c = jnp.einsum('bgrqn,bgnd->bgrqd', p_c, v_cmp)
        imp = jnp.einsum('bgrqn,ns->bgqs', p_c, overlap)
        cur = t // NSA_SLC_BLOCK
        valid = blk_ids[None, :] <= cur[:, None]
        forced = (blk_ids[None, :] == 0) | (blk_ids[None, :] >= cur[:, None] - (NSA_LOCAL_BLOCKS - 1))
        score = jnp.where(valid, jnp.where(forced, BIG, imp), -BIG)
        _, sel = lax.top_k(score, top_n)
        tok = (sel[..., None] * NSA_SLC_BLOCK + sel_offsets).reshape(bsz, G, -1)
        k_s = gather(k_slc, tok).reshape(bsz, G, NSA_Q_CHUNK, top_n * NSA_SLC_BLOCK, dh)
        v_s = gather(v_slc, tok).reshape(bsz, G, NSA_Q_CHUNK, top_n * NSA_SLC_BLOCK, dh)
        tok = tok.reshape(bsz, G, 1, NSA_Q_CHUNK, top_n * NSA_SLC_BLOCK)
        s_s = jnp.einsum('bgrqd,bgqkd->bgrqk', qc, k_s) * scale
        p_s = masked_softmax(s_s, tok <= t[:, None])
        o_s = jnp.einsum('bgrqk,bgqkd->bgrqd', p_s, v_s)
        kw = lax.dynamic_slice_in_dim(k_win, q0, NSA_Q_CHUNK + NSA_WINDOW, axis=2)
        vw = lax.dynamic_slice_in_dim(v_win, q0, NSA_Q_CHUNK + NSA_WINDOW, axis=2)
        kpos = q0 - NSA_WINDOW + jnp.arange(NSA_Q_CHUNK + NSA_WINDOW)
        rel = t[:, None] - kpos[None, :]
        mask_w = (rel >= 0) & (rel < NSA_WINDOW) & (kpos[None, :] >= 0)
        s_w = jnp.einsum('bgrqd,bgkd->bgrqk', qc, kw) * scale
        p_w = masked_softmax(s_w, mask_w)
        o_w = jnp.einsum('bgrqk,bgkd->bgrqd', p_w, vw)
        g = lax.dynamic_slice_in_dim(gates, q0, NSA_Q_CHUNK, axis=1)
        g = g.transpose(2, 0, 3, 1).reshape(3, bsz, G, R, NSA_Q_CHUNK, 1)
        return g[0] * o_c + g[1] * o_s + g[2] * o_w

    o = lax.map(chunk, jnp.arange(seq // NSA_Q_CHUNK))
    o = o.transpose(1, 0, 4, 2, 3, 5).reshape(bsz, seq, H * dh)
    return o.astype(h.dtype) @ w_out


def _linear_recurrence(e1, e2):
    a1, b1 = e1
    a2, b2 = e2
    return a1 * a2, a2 * b1 + b2


def s5_ssm(h, lam_re, lam_im, log_step, b_re, b_im, c_re, c_im, d_skip, w_glu, b_glu):
    bsz, seq, _ = h.shape
    f32 = jnp.float32
    lam = lax.complex(lam_re.astype(f32), lam_im.astype(f32))
    step = jnp.exp(log_step.astype(f32))[:, None]
    lam_bar = jnp.exp(lam * step)
    b_bar = ((lam_bar - 1.0) / lam)[:, :, None] * lax.complex(b_re.astype(f32), b_im.astype(f32))
    c_mat = lax.complex(c_re.astype(f32), c_im.astype(f32))

    def run_sequence(u):
        ug = u.astype(f32).reshape(seq, S5_N_GROUPS, S5_GROUP)
        bu = jnp.einsum('gpc,sgc->sgp', b_bar, ug)
        a = jnp.broadcast_to(lam_bar, bu.shape)
        _, states = lax.associative_scan(_linear_recurrence, (a, bu), axis=0)
        return jnp.einsum('gcp,sgp->sgc', c_mat, states).real.reshape(seq, D_MODEL)

    y = lax.map(run_sequence, h) + d_skip.astype(f32) * h.astype(f32)
    y = jax.nn.gelu(y).astype(h.dtype)
    a, g = jnp.split(y @ w_glu + b_glu, 2, axis=-1)
    return a * jax.nn.sigmoid(g)


def multiscale_pool(h, w_grp, scale):
    bsz, seq, _ = h.shape
    f32 = jnp.float32
    hf = h.astype(f32)
    csum = jnp.concatenate([jnp.zeros((bsz, 1, D_MODEL), f32), lax.cumsum(hf, axis=1)], axis=1)
    t = jnp.arange(seq)
    outs = []
    for gi, win in enumerate(POOL_WINDOWS):
        lo, hi = gi * POOL_GROUP, (gi + 1) * POOL_GROUP
        start = jnp.maximum(t + 1 - win, 0)
        cnt = (t + 1 - start).astype(f32)[None, :, None]
        pooled = (csum[:, 1:, lo:hi] - csum[:, start, lo:hi]) / cnt - hf[:, :, lo:hi]
        outs.append(jnp.einsum('bsc,cd->bsd', pooled, w_grp[gi].astype(f32)))
    return (jnp.concatenate(outs, axis=-1) * scale.astype(f32)).astype(h.dtype)


def setup_inputs(seed: int = 0) -> dict:
    key = jax.random.key(seed)
    keys = iter(jax.random.split(key, 64))
    f32 = jnp.float32

    def nrm(shape, s):
        return jax.random.normal(next(keys), shape, f32) * s

    nA, nB, nC, nD = [_n_layers_of(m) for m in range(N_MIXERS)]
    D, F = D_MODEL, D_FF
    H, G, dh, L = NSA_HEADS, NSA_KV_GROUPS, NSA_HEAD_DIM, NSA_CMP_BLOCK
    Gs, P, C = S5_N_GROUPS, S5_STATE, S5_GROUP
    inv2 = math.sqrt(0.5)
    return {
        'x': nrm((BATCH, SEQ, D), 1.0),
        'positions': jnp.arange(SEQ, dtype=jnp.int32),
        'norm_mix': 1.0 + nrm((DEPTH, D), 0.02),
        'norm_ffn': 1.0 + nrm((DEPTH, D), 0.02),
        'ffn_w_in': nrm((DEPTH, D, 2 * F), D ** -0.5),
        'ffn_w_out': nrm((DEPTH, F, D), F ** -0.5),
        'conv_w_in': nrm((nA, D, 2 * D), D ** -0.5),
        'conv_b_in': nrm((nA, 2 * D), 0.02),
        'conv_w_dw': nrm((nA, CONV_WIDTH, D), CONV_WIDTH ** -0.5),
        'conv_b_dw': nrm((nA, D), 0.02),
        'conv_ln_g': 1.0 + nrm((nA, D), 0.02),
        'conv_ln_b': nrm((nA, D), 0.02),
        'conv_w_out': nrm((nA, D, D), D ** -0.5),
        'nsa_w_in': nrm((nB, D, NSA_PROJ), D ** -0.5),
        'nsa_q_gain': 1.0 + nrm((nB, dh), 0.02),
        'nsa_k_gain': 1.0 + nrm((nB, 3, dh), 0.02),
        'nsa_cmp_pos': nrm((nB, 2, L, dh), 0.02),
        'nsa_cmp_w1': nrm((nB, 2, L * dh, dh), (L * dh) ** -0.5),
        'nsa_cmp_b1': nrm((nB, 2, dh), 0.02),
        'nsa_cmp_w2': nrm((nB, 2, dh, dh), dh ** -0.5),
        'nsa_w_out': nrm((nB, H * dh, D), (H * dh) ** -0.5),
        's5_lam_re': jnp.full((nC, Gs, P), -0.5, f32),
        's5_lam_im': jnp.broadcast_to(math.pi * jnp.arange(P, dtype=f32), (nC, Gs, P)),
        's5_log_step': jax.random.uniform(next(keys), (nC, Gs), f32, math.log(1e-3), math.log(1e-1)),
        's5_b_re': nrm((nC, Gs, P, C), inv2 * C ** -0.5),
        's5_b_im': nrm((nC, Gs, P, C), inv2 * C ** -0.5),
        's5_c_re': nrm((nC, Gs, C, P), inv2 * P ** -0.5),
        's5_c_im': nrm((nC, Gs, C, P), inv2 * P ** -0.5),
        's5_d': nrm((nC, D), 0.5),
        's5_w_glu': nrm((nC, D, 2 * D), D ** -0.5),
        's5_b_glu': nrm((nC, 2 * D), 0.02),
        'pool_w': nrm((nD, len(POOL_WINDOWS), POOL_GROUP, POOL_GROUP), POOL_GROUP ** -0.5),
        'pool_scale': 1.0 + nrm((nD, D), 0.05),
    }


def reference(x, positions, norm_mix, norm_ffn, ffn_w_in, ffn_w_out,
              conv_w_in, conv_b_in, conv_w_dw, conv_b_dw, conv_ln_g, conv_ln_b, conv_w_out,
              nsa_w_in, nsa_q_gain, nsa_k_gain, nsa_cmp_pos, nsa_cmp_w1, nsa_cmp_b1, nsa_cmp_w2, nsa_w_out,
              s5_lam_re, s5_lam_im, s5_log_step, s5_b_re, s5_b_im, s5_c_re, s5_c_im, s5_d, s5_w_glu, s5_b_glu,
              pool_w, pool_scale):
    for i in range(DEPTH):
        m, j = i % N_MIXERS, i // N_MIXERS
        h = rmsnorm(x, norm_mix[i])
        if m == 0:
            y = conformer_conv_module(h, conv_w_in[j], conv_b_in[j], conv_w_dw[j], conv_b_dw[j],
                                      conv_ln_g[j], conv_ln_b[j], conv_w_out[j])
        elif m == 1:
            y = nsa_attention(h, positions, nsa_w_in[j], nsa_q_gain[j], nsa_k_gain[j], nsa_cmp_pos[j],
                              nsa_cmp_w1[j], nsa_cmp_b1[j], nsa_cmp_w2[j], nsa_w_out[j])
        elif m == 2:
            y = s5_ssm(h, s5_lam_re[j], s5_lam_im[j], s5_log_step[j], s5_b_re[j], s5_b_im[j],
                       s5_c_re[j], s5_c_im[j], s5_d[j], s5_w_glu[j], s5_b_glu[j])
        else:
            y = multiscale_pool(h, pool_w[j], pool_scale[j])
        x = x + y.astype(x.dtype)
        h = rmsnorm(x, norm_ffn[i])
        x = x + swiglu_ffn(h, ffn_w_in[i], ffn_w_out[i]).astype(x.dtype)
    return x
```

```python
import functools
import math

import numpy as np
import jax
import jax.numpy as jnp
from jax import lax
from jax.experimental import pallas as pl
from jax.experimental.pallas import tpu as pltpu

F32 = jnp.float32
BF16 = jnp.bfloat16

RMS_EPS = 1e-6
LN_EPS = 1e-5
N_MIXERS = 4
CONV_WIDTH = 31
POOL_WINDOWS = (2, 4, 8, 16)

V7X_VMEM_LIMIT_BYTES = 56 * 1024 * 1024
SUBLANES = 8
LANES = 128


def _cp(*sem):
    return pltpu.CompilerParams(dimension_semantics=sem, vmem_limit_bytes=V7X_VMEM_LIMIT_BYTES)


def _const_spec(shape):
    nd = len(shape)
    return pl.BlockSpec(shape, lambda *_: (0,) * nd, pipeline_mode=pl.Buffered(1))


def _dot(a, b):
    return jnp.dot(a, b, preferred_element_type=F32)


def _rms(x, g):
    ms = jnp.mean(x * x, axis=-1, keepdims=True)
    return x * lax.rsqrt(ms + RMS_EPS) * g


def _row(v):
    return v.reshape(1, -1).astype(F32)


FFN_CHUNK = 256


def _ffn_body(x_ref, g_ref, wg_ref, wu_ref, wo_ref, o_ref, act_sc):
    x = x_ref[...]
    h = _rms(x, g_ref[...]).astype(BF16)
    n_chunks = act_sc.shape[1] // FFN_CHUNK
    for c in range(n_chunks):
        sl = slice(c * FFN_CHUNK, (c + 1) * FFN_CHUNK)
        gate = _dot(h, wg_ref[:, sl])
        up = _dot(h, wu_ref[:, sl])
        act_sc[:, sl] = (gate * jax.nn.sigmoid(gate) * up).astype(BF16)
    o_ref[...] = x + _dot(act_sc[...], wo_ref[...])


def _ffn(x2, g, w_in, w_out, tm=512):
    t, d = x2.shape
    f = w_out.shape[0]
    w_in = w_in.astype(BF16)
    w_out = w_out.astype(BF16)
    return pl.pallas_call(
        _ffn_body,
        grid=(t // tm,),
        in_specs=[
            pl.BlockSpec((tm, d), lambda i: (i, 0)),
            _const_spec((1, d)),
            pl.BlockSpec((d, f), lambda i: (0, 0), pipeline_mode=pl.Buffered(1)),
            pl.BlockSpec((d, f), lambda i: (0, 1), pipeline_mode=pl.Buffered(1)),
            _const_spec((f, d)),
        ],
        out_specs=pl.BlockSpec((tm, d), lambda i: (i, 0)),
        out_shape=jax.ShapeDtypeStruct((t, d), F32),
        scratch_shapes=[pltpu.VMEM((tm, f), BF16)],
        compiler_params=_cp("parallel"),
        name="ffn",
    )(x2, _row(g), w_in, w_in, w_out)


CONV_HALO = 32
CONV_ROWS = 32


def _conv_body(x_ref, g_ref, wa_ref, wg_ref, ba_ref, bg_ref, wdw_ref, bdw_ref,
               lng_ref, lnb_ref, wo_ref, o_ref, u_sc, v_sc):
    tm, d = x_ref.shape
    s = pl.program_id(1)

    @pl.when(s == 0)
    def _():
        u_sc[0:CONV_HALO, :] = jnp.zeros((CONV_HALO, d), F32)

    @pl.when(s > 0)
    def _():
        u_sc[0:CONV_HALO, :] = u_sc[tm:tm + CONV_HALO, :]

    x = x_ref[...]
    h = _rms(x, g_ref[...]).astype(BF16)
    a = _dot(h, wa_ref[...]) + ba_ref[...]
    gt = _dot(h, wg_ref[...]) + bg_ref[...]
    u_sc[CONV_HALO:CONV_HALO + tm, :] = a * jax.nn.sigmoid(gt)

    off = CONV_HALO - (CONV_WIDTH - 1)
    for r0 in range(0, tm, CONV_ROWS):
        acc = jnp.zeros((CONV_ROWS, d), F32) + bdw_ref[...]
        for k in range(CONV_WIDTH):
            acc = acc + wdw_ref[k:k + 1, :] * u_sc[r0 + k + off:r0 + k + off + CONV_ROWS, :]
        v_sc[r0:r0 + CONV_ROWS, :] = acc

    v = v_sc[...]
    mu = jnp.mean(v, axis=-1, keepdims=True)
    vc = v - mu
    var = jnp.mean(vc * vc, axis=-1, keepdims=True)
    y = vc * lax.rsqrt(var + LN_EPS) * lng_ref[...] + lnb_ref[...]
    y = (y * jax.nn.sigmoid(y)).astype(BF16)
    o_ref[...] = x + _dot(y, wo_ref[...])


def _conv_layer(x3, g, w_in, b_in, w_dw, b_dw, ln_g, ln_b, w_out, tm=256):
    b, s, d = x3.shape
    w_in = w_in.astype(BF16)
    return pl.pallas_call(
        _conv_body,
        grid=(b, s // tm),
        in_specs=[
            pl.BlockSpec((None, tm, d), lambda i, j: (i, j, 0)),
            _const_spec((1, d)),
            pl.BlockSpec((d, d), lambda i, j: (0, 0), pipeline_mode=pl.Buffered(1)),
            pl.BlockSpec((d, d), lambda i, j: (0, 1), pipeline_mode=pl.Buffered(1)),
            pl.BlockSpec((1, d), lambda i, j: (0, 0), pipeline_mode=pl.Buffered(1)),
            pl.BlockSpec((1, d), lambda i, j: (0, 1), pipeline_mode=pl.Buffered(1)),
            _const_spec((CONV_WIDTH, d)),
            _const_spec((1, d)),
            _const_spec((1, d)),
            _const_spec((1, d)),
            _const_spec((d, d)),
        ],
        out_specs=pl.BlockSpec((None, tm, d), lambda i, j: (i, j, 0)),
        out_shape=jax.ShapeDtypeStruct((b, s, d), F32),
        scratch_shapes=[pltpu.VMEM((CONV_HALO + tm, d), F32), pltpu.VMEM((tm, d), F32)],
        compiler_params=_cp("parallel", "arbitrary"),
        name="conv_module",
    )(x3, _row(g), w_in, w_in, _row(b_in), _row(b_in), w_dw.astype(F32), _row(b_dw),
      _row(ln_g), _row(ln_b), w_out.astype(BF16))


POOL_HALO = 16


def _pool_body(x_ref, g_ref, w_ref, sc_ref, o_ref, h_sc):
    tm, d = x_ref.shape
    s = pl.program_id(1)
    grp = d // len(POOL_WINDOWS)

    @pl.when(s == 0)
    def _():
        h_sc[0:POOL_HALO, :] = jnp.zeros((POOL_HALO, d), F32)

    @pl.when(s > 0)
    def _():
        h_sc[0:POOL_HALO, :] = h_sc[tm:tm + POOL_HALO, :]

    x = x_ref[...]
    h_sc[POOL_HALO:POOL_HALO + tm, :] = _rms(x, g_ref[...])
    t = s * tm + lax.broadcasted_iota(jnp.int32, (tm, 1), 0)
    for gi, win in enumerate(POOL_WINDOWS):
        lo = gi * grp
        hcur = h_sc[POOL_HALO:POOL_HALO + tm, lo:lo + grp]
        tot = hcur
        for k in range(1, win):
            tot = tot + h_sc[POOL_HALO - k:POOL_HALO - k + tm, lo:lo + grp]
        cnt = jnp.minimum(t + 1, win).astype(F32)
        pooled = (tot * (1.0 / cnt) - hcur).astype(BF16)
        y = _dot(pooled, w_ref[gi]) * sc_ref[:, lo:lo + grp]
        o_ref[:, lo:lo + grp] = x[:, lo:lo + grp] + y


def _pool_layer(x3, g, w_grp, scale, tm=512):
    b, s, d = x3.shape
    ng, grp, _ = w_grp.shape
    return pl.pallas_call(
        _pool_body,
        grid=(b, s // tm),
        in_specs=[
            pl.BlockSpec((None, tm, d), lambda i, j: (i, j, 0)),
            _const_spec((1, d)),
            _const_spec((ng, grp, grp)),
            _const_spec((1, d)),
        ],
        out_specs=pl.BlockSpec((None, tm, d), lambda i, j: (i, j, 0)),
        out_shape=jax.ShapeDtypeStruct((b, s, d), F32),
        scratch_shapes=[pltpu.VMEM((POOL_HALO + tm, d), F32)],
        compiler_params=_cp("parallel", "arbitrary"),
        name="pool",
    )(x3, _row(g), w_grp.astype(BF16), _row(scale))


S5_L = 8
S5_GPT = 8


def _s5_operators(lam_re, lam_im, log_step, b_re, b_im, c_re, c_im):
    hi = lax.Precision.HIGHEST
    n_g, n_p = lam_re.shape
    n_c = b_re.shape[-1]
    n_j = n_g // S5_GPT
    L = S5_L
    lam = lax.complex(lam_re.astype(F32), lam_im.astype(F32))
    step = jnp.exp(log_step.astype(F32))[:, None]
    lam_bar = jnp.exp(lam * step)
    b_bar = ((lam_bar - 1.0) / lam)[:, :, None] * lax.complex(b_re.astype(F32), b_im.astype(F32))
    c_mat = lax.complex(c_re.astype(F32), c_im.astype(F32))
    dd = jnp.arange(2 * L, dtype=F32)
    pw = jnp.exp((lam * step)[None] * dd[:, None, None].astype(jnp.complex64))
    eye = jnp.eye(S5_GPT, dtype=F32)

    kern = jnp.real(jnp.einsum('gap,dgp,gpc->gdac', c_mat, pw[:L], b_bar, precision=hi))
    sig = np.arange(L)[:, None]
    tau = np.arange(L)[None, :]
    lag = np.clip(tau - sig, 0, L - 1)
    kt = kern[:, lag] * jnp.asarray((tau >= sig).astype(np.float32))[None, :, :, None, None]
    kt = kt.reshape(n_j, S5_GPT, L, L, n_c, n_c)
    m_op = jnp.einsum('jgstac,gh->jsgctha', kt, eye).reshape(n_j, L * S5_GPT * n_c, L * S5_GPT * n_c)

    e = pw[:L][::-1].transpose(1, 0, 2)[:, :, :, None] * b_bar[:, None]
    e = jnp.stack([jnp.real(e), jnp.imag(e)], axis=0).reshape(2, n_j, S5_GPT, L, n_p, n_c)
    b_op = jnp.einsum('rjgspc,gh->jsgcrhp', e, eye).reshape(n_j, L * S5_GPT * n_c, 2 * S5_GPT * n_p)

    f = c_mat[:, None] * pw[1:L + 1].transpose(1, 0, 2)[:, :, None, :]
    f = jnp.stack([jnp.real(f), -jnp.imag(f)], axis=0).reshape(2, n_j, S5_GPT, L, n_c, n_p)
    c_op = jnp.einsum('rjgtap,gh->jrgptha', f, eye).reshape(n_j, 2 * S5_GPT * n_p, L * S5_GPT * n_c)

    ak = jnp.exp((lam * step * L)[None] * dd[:, None, None].astype(jnp.complex64))
    ak = jnp.stack([jnp.real(ak), jnp.imag(ak)], axis=1).reshape(2 * L, 2, n_j, S5_GPT * n_p)
    a_pw = ak.transpose(2, 0, 1, 3).reshape(n_j, 2 * L, 2 * S5_GPT * n_p)
    return m_op.astype(BF16), b_op.astype(BF16), c_op.astype(BF16), a_pw.astype(F32)


def _s5_regroup_body(x_ref, g_ref, o_ref, u_sc):
    tm, d = x_ref.shape
    u = _rms(x_ref[...], g_ref[...])
    rows = tm // S5_L
    for j in range(d // LANES):
        u_sc[j] = u[:, j * LANES:(j + 1) * LANES]
    for j in range(d // LANES):
        for sg in range(S5_L):
            o_ref[j, :, sg * LANES:(sg + 1) * LANES] = (
                u_sc[j, pl.ds(sg, rows, stride=S5_L), :].astype(BF16))


def _s5_regroup(x2, g, tm=512):
    t, d = x2.shape
    n_j = d // LANES
    return pl.pallas_call(
        _s5_regroup_body,
        grid=(t // tm,),
        in_specs=[pl.BlockSpec((tm, d), lambda i: (i, 0)), _const_spec((1, d))],
        out_specs=pl.BlockSpec((n_j, tm // S5_L, S5_L * LANES), lambda i: (0, i, 0)),
        out_shape=jax.ShapeDtypeStruct((n_j, t // S5_L, S5_L * LANES), BF16),
        scratch_shapes=[pltpu.VMEM((n_j, tm, LANES), F32)],
        compiler_params=_cp("parallel"),
        name="s5_regroup",
    )(x2, _row(g))


def _s5_chunk_body(u_ref, m_ref, b_ref, c_ref, a_ref, y_ref, xr_sc, xi_sc, pr_sc, pi_sc, cr_sc, ci_sc):
    rows, width = u_ref.shape
    half = width // 2
    nt = pl.program_id(2)

    @pl.when(nt == 0)
    def _():
        cr_sc[...] = jnp.zeros_like(cr_sc)
        ci_sc[...] = jnp.zeros_like(ci_sc)

    u = u_ref[...]
    xin = _dot(u, b_ref[...])
    xr, xi = xin[:, :half], xin[:, half:]
    row = lax.broadcasted_iota(jnp.int32, (rows, 1), 0) % SUBLANES
    for shift in (1, 2, 4):
        ar, ai = a_ref[shift:shift + 1, :half], a_ref[shift:shift + 1, half:]
        keep = row >= shift
        sr = jnp.where(keep, pltpu.roll(xr, shift, axis=0), 0.0)
        si = jnp.where(keep, pltpu.roll(xi, shift, axis=0), 0.0)
        xr, xi = xr + ar * sr - ai * si, xi + ar * si + ai * sr
    xr_sc[...] = xr
    xi_sc[...] = xi
    keep = row >= 1
    pr_sc[...] = jnp.where(keep, pltpu.roll(xr, 1, axis=0), 0.0)
    pi_sc[...] = jnp.where(keep, pltpu.roll(xi, 1, axis=0), 0.0)

    akr, aki = a_ref[0:SUBLANES, :half], a_ref[0:SUBLANES, half:]
    a8r, a8i = a_ref[SUBLANES:SUBLANES + 1, :half], a_ref[SUBLANES:SUBLANES + 1, half:]

    def tile_step(t, carry):
        cr, ci = carry
        base = pl.multiple_of(t * SUBLANES, SUBLANES)
        pr_sc[pl.ds(base, SUBLANES), :] = pr_sc[pl.ds(base, SUBLANES), :] + akr * cr - aki * ci
        pi_sc[pl.ds(base, SUBLANES), :] = pi_sc[pl.ds(base, SUBLANES), :] + akr * ci + aki * cr
        lr = xr_sc[pl.ds(base + SUBLANES - 1, 1), :]
        li = xi_sc[pl.ds(base + SUBLANES - 1, 1), :]
        return lr + a8r * cr - a8i * ci, li + a8r * ci + a8i * cr

    cr, ci = lax.fori_loop(0, rows // SUBLANES, tile_step, (cr_sc[...], ci_sc[...]))
    cr_sc[...] = cr
    ci_sc[...] = ci

    y = _dot(u, m_ref[...])
    y = y + _dot(pr_sc[...].astype(BF16), c_ref[0:half, :])
    y = y + _dot(pi_sc[...].astype(BF16), c_ref[half:, :])
    y_ref[...] = y


def _s5_chunks(u3, m_op, b_op, c_op, a_pw, bsz):
    n_j, n_rows, width = u3.shape
    rows_per_seq = n_rows // bsz
    rows = min(512, rows_per_seq)
    nt = rows_per_seq // rows
    half = width // 2
    wspec = lambda shape: pl.BlockSpec((None,) + shape, lambda j, b, t: (j, 0, 0), pipeline_mode=pl.Buffered(1))
    return pl.pallas_call(
        _s5_chunk_body,
        grid=(n_j, bsz, nt),
        in_specs=[
            pl.BlockSpec((None, rows, width), lambda j, b, t: (j, b * nt + t, 0)),
            wspec((width, width)), wspec((width, width)), wspec((width, width)),
            wspec((2 * S5_L, width)),
        ],
        out_specs=pl.BlockSpec((None, rows, width), lambda j, b, t: (j, b * nt + t, 0)),
        out_shape=jax.ShapeDtypeStruct((n_j, n_rows, width), F32),
        scratch_shapes=[pltpu.VMEM((rows, half), F32)] * 4 + [pltpu.VMEM((1, half), F32)] * 2,
        compiler_params=_cp("arbitrary", "arbitrary", "arbitrary"),
        name="s5_chunks",
    )(u3, m_op, b_op, c_op, a_pw)


def _s5_out_body(x_ref, y3_ref, g_ref, d_ref, w_ref, b_ref, o_ref, y_sc):
    tm, d = x_ref.shape
    rows = tm // S5_L
    for j in range(d // LANES):
        for tau in range(S5_L):
            y_sc[j, pl.ds(tau, rows, stride=S5_L), :] = y3_ref[j, :, tau * LANES:(tau + 1) * LANES]
    x = x_ref[...]
    u = _rms(x, g_ref[...])
    y_ssm = jnp.concatenate([y_sc[j] for j in range(d // LANES)], axis=1)
    y = jax.nn.gelu(y_ssm + d_ref[...] * u).astype(BF16)
    z = _dot(y, w_ref[...]) + b_ref[...]
    o_ref[...] = x + z[:, :d] * jax.nn.sigmoid(z[:, d:])


def _s5_out(x2, y3, g, d_skip, w_glu, b_glu, tm=512):
    t, d = x2.shape
    n_j = d // LANES
    return pl.pallas_call(
        _s5_out_body,
        grid=(t // tm,),
        in_specs=[
            pl.BlockSpec((tm, d), lambda i: (i, 0)),
            pl.BlockSpec((n_j, tm // S5_L, S5_L * LANES), lambda i: (0, i, 0)),
            _const_spec((1, d)), _const_spec((1, d)),
            _const_spec((d, 2 * d)), _const_spec((1, 2 * d)),
        ],
        out_specs=pl.BlockSpec((tm, d), lambda i: (i, 0)),
        out_shape=jax.ShapeDtypeStruct((t, d), F32),
        scratch_shapes=[pltpu.VMEM((n_j, tm, LANES), F32)],
        compiler_params=_cp("parallel"),
        name="s5_out",
    )(x2, y3, _row(g), _row(d_skip), w_glu.astype(BF16), _row(b_glu))


def _s5_layer(x3, g, lam_re, lam_im, log_step, b_re, b_im, c_re, c_im, d_skip, w_glu, b_glu):
    b, s, d = x3.shape
    x2 = x3.reshape(b * s, d)
    m_op, b_op, c_op, a_pw = _s5_operators(lam_re, lam_im, log_step, b_re, b_im, c_re, c_im)
    u3 = _s5_regroup(x2, g)
    y3 = _s5_chunks(u3, m_op, b_op, c_op, a_pw, b)
    return _s5_out(x2, y3, g, d_skip, w_glu, b_glu).reshape(b, s, d)


NSA_HEAD_DIM = 64
NSA_HEADS = 16
NSA_KV_GROUPS = 4
NSA_REP = NSA_HEADS // NSA_KV_GROUPS
NSA_CMP_BLOCK = 32
NSA_CMP_STRIDE = 16
NSA_SLC_BLOCK = 64
NSA_TOP_N = 16
NSA_LOCAL_BLOCKS = 2
NSA_WINDOW = 512
ROPE_THETA = 500000.0
ROPE_DIMS = NSA_HEAD_DIM // 4
NEG_INF = -1e30
BIG = 1e9
NSA_MAX_SEL_BLOCKS = 64
NSA_NORM_SECTIONS = 8


def _rope_body(pos_ref, inv_ref, c_ref, s1_ref, s2_ref):
    ang = pos_ref[...].astype(F32) * inv_ref[...]
    lane = lax.broadcasted_iota(jnp.int32, ang.shape, 1) % NSA_HEAD_DIM
    cos, sin = jnp.cos(ang), jnp.sin(ang)
    half = ROPE_DIMS // 2
    c_ref[...] = jnp.where(lane < ROPE_DIMS, cos, 1.0)
    s1_ref[...] = jnp.where(lane < half, -sin, 0.0)
    s2_ref[...] = jnp.where(lane < half, 0.0, jnp.where(lane < ROPE_DIMS, sin, 0.0))


def _rope_tables(positions):
    s = positions.shape[0]
    half = ROPE_DIMS // 2
    inv_freq = ROPE_THETA ** (-jnp.arange(half, dtype=F32) / half)
    lane = np.arange(LANES) % NSA_HEAD_DIM
    inv_lane = jnp.where(jnp.asarray(lane < ROPE_DIMS), inv_freq[lane % half], 0.0).reshape(1, LANES)
    ts = min(s, 512)
    spec = pl.BlockSpec((ts, LANES), lambda i: (i, 0))
    return pl.pallas_call(
        _rope_body,
        grid=(s // ts,),
        in_specs=[pl.BlockSpec((ts, 1), lambda i: (i, 0)), _const_spec((1, LANES))],
        out_specs=[spec, spec, spec],
        out_shape=[jax.ShapeDtypeStruct((s, LANES), F32)] * 3,
        compiler_params=_cp("parallel"),
        name="rope_tables",
    )(positions.reshape(s, 1), inv_lane)


def _nsa_proj_body(x_ref, g_ref, wn_ref, wv_ref, wg_ref, gain_ref, bd_ref, c_ref, s1_ref, s2_ref,
                   q_ref, kc_ref, kw_ref, ksa_ref, vc_ref, vs_ref, vw_ref, gt_ref):
    tm, d = x_ref.shape
    sw = 2 * LANES
    s_idx = pl.program_id(1)
    h = _rms(x_ref[...], g_ref[...]).astype(BF16)
    c2 = jnp.concatenate([c_ref[...]] * 2, axis=1)
    s1 = jnp.concatenate([s1_ref[...]] * 2, axis=1)
    s2 = jnp.concatenate([s2_ref[...]] * 2, axis=1)
    blk = (s_idx * tm + lax.broadcasted_iota(jnp.int32, (tm, 1), 0)) // NSA_SLC_BLOCK
    lane = lax.broadcasted_iota(jnp.int32, (tm, sw), 1) % LANES
    blk_id = jnp.where(lane - NSA_HEAD_DIM == blk, NEG_INF, 0.0)
    inv_dh = 1.0 / NSA_HEAD_DIM
    for sec in range(NSA_NORM_SECTIONS):
        z = _dot(h, wn_ref[:, sec * sw:(sec + 1) * sw])
        aug = sec >= 6
        ss = _dot((z * z).astype(BF16), bd_ref[1 if aug else 0]) * inv_dh
        z = z * lax.rsqrt(ss + RMS_EPS) * gain_ref[:, sec * sw:(sec + 1) * sw]
        z = z * c2 + pltpu.roll(z, sw - ROPE_DIMS // 2, axis=1) * s1 + pltpu.roll(z, ROPE_DIMS // 2, axis=1) * s2
        if sec < 4:
            q_ref[:, sec * sw:(sec + 1) * sw] = z.astype(BF16)
        elif sec == 4:
            kc_ref[...] = z.astype(BF16)
        elif sec == 5:
            for g in range(NSA_KV_GROUPS):
                kw_ref[g] = z[:, g * NSA_HEAD_DIM:(g + 1) * NSA_HEAD_DIM].astype(BF16)
        else:
            ksa_ref[:, (sec - 6) * sw:(sec - 5) * sw] = (z + blk_id).astype(BF16)
    v = _dot(h, wv_ref[...])
    vc_ref[...] = v[:, 0:sw].astype(BF16)
    for g in range(NSA_KV_GROUPS):
        vs_ref[g] = v[:, sw + g * NSA_HEAD_DIM:sw + (g + 1) * NSA_HEAD_DIM].astype(BF16)
        vw_ref[g] = v[:, 2 * sw + g * NSA_HEAD_DIM:2 * sw + (g + 1) * NSA_HEAD_DIM].astype(BF16)
    gt_ref[...] = jax.nn.sigmoid(_dot(h, wg_ref[...]))


def _nsa_proj(x3, g_mix, w_in, q_gain, k_gain, rope, tm=512):
    b, s, d = x3.shape
    hh, gg, dh = NSA_HEADS, NSA_KV_GROUPS, NSA_HEAD_DIM
    qd, gw = hh * dh, gg * dh
    tm = min(tm, s)
    wq = w_in[:, :qd]
    wkv = w_in[:, qd:qd + 6 * gw].reshape(d, 3, 2, gw)
    wgate = jnp.pad(w_in[:, qd + 6 * gw:], ((0, 0), (0, LANES - 3 * hh)))

    def spread(a):
        a = a.reshape(a.shape[:-1] + (gg, dh))
        return jnp.pad(a, [(0, 0)] * (a.ndim - 1) + [(0, LANES - dh)]).reshape(a.shape[:-2] + (gg * LANES,))

    w_norm = jnp.concatenate([wq, wkv[:, 0, 0], wkv[:, 2, 0], spread(wkv[:, 1, 0])], axis=1).astype(BF16)
    w_v = jnp.concatenate([wkv[:, 0, 1], wkv[:, 1, 1], wkv[:, 2, 1]], axis=1).astype(BF16)
    gain = jnp.concatenate([jnp.tile(q_gain, hh) * (dh ** -0.5), jnp.tile(k_gain[0], gg), jnp.tile(k_gain[2], gg),
                            spread(jnp.tile(k_gain[1], gg))]).reshape(1, -1).astype(F32)
    lane = np.arange(2 * LANES)
    bd = np.stack([(lane[:, None] // dh == lane[None, :] // dh), (lane[:, None] // LANES == lane[None, :] // LANES)])
    bd = jnp.asarray(bd.astype(np.float32), BF16)
    n_norm = NSA_NORM_SECTIONS * 2 * LANES
    rope_spec = pl.BlockSpec((tm, LANES), lambda i, j: (j, 0))
    tok = lambda w: pl.BlockSpec((None, tm, w), lambda i, j: (i, j, 0))
    grp = pl.BlockSpec((None, gg, tm, dh), lambda i, j: (i, 0, j, 0))
    sds = jax.ShapeDtypeStruct
    return pl.pallas_call(
        _nsa_proj_body,
        grid=(b, s // tm),
        in_specs=[tok(d), _const_spec((1, d)), _const_spec((d, n_norm)), _const_spec((d, 3 * gw)),
                  _const_spec((d, LANES)), _const_spec((1, n_norm)), _const_spec((2, 2 * LANES, 2 * LANES)),
                  rope_spec, rope_spec, rope_spec],
        out_specs=[tok(qd), tok(gw), grp, tok(gg * LANES), tok(gw), grp, grp, tok(LANES)],
        out_shape=[sds((b, s, qd), BF16), sds((b, s, gw), BF16), sds((b, gg, s, dh), BF16),
                   sds((b, s, gg * LANES), BF16), sds((b, s, gw), BF16), sds((b, gg, s, dh), BF16),
                   sds((b, gg, s, dh), BF16), sds((b, s, LANES), F32)],
        compiler_params=_cp("parallel", "parallel"),
        name="nsa_proj",
    )(x3, _row(g_mix), w_norm, w_v, wgate.astype(BF16), gain, bd, *rope)


def _nsa_compress_body(kv_ref, wa_ref, wb_ref, pa_ref, pb_ref, b1_ref, w2_ref, o_ref, q_sc, *, n_cmp):
    nr = kv_ref.shape[1]
    gw = o_ref.shape[-1]
    row = lax.broadcasted_iota(jnp.int32, (nr, 1), 0)
    for c in range(2):
        x = kv_ref[c].astype(F32)
        first = _dot((x + pa_ref[c]).astype(BF16), wa_ref[c])
        q_sc[0:nr, :] = _dot((x + pb_ref[c]).astype(BF16), wb_ref[c])
        q_sc[nr:nr + SUBLANES, :] = jnp.zeros((SUBLANES, q_sc.shape[1]), F32)
        hid = jax.nn.gelu(first + q_sc[1:nr + 1, :] + b1_ref[c])
        comp = jnp.where(row < n_cmp, _dot(hid.astype(BF16), w2_ref[c]), 0.0)
        for g in range(NSA_KV_GROUPS):
            o_ref[c, g] = comp[:, g * NSA_HEAD_DIM:(g + 1) * NSA_HEAD_DIM].astype(BF16)


def _nsa_compress(kc, vc, cmp_pos, cmp_w1, cmp_b1, cmp_w2):
    b, s, gw = kc.shape
    gg, dh, st = NSA_KV_GROUPS, NSA_HEAD_DIM, NSA_CMP_STRIDE
    nr = s // st
    n_cmp = (s - NSA_CMP_BLOCK) // st + 1
    kv = jnp.stack([kc, vc]).reshape(2, b, nr, st * gw)
    eye = jnp.eye(gg, dtype=F32)
    w1 = cmp_w1.reshape(2, 2, st, dh, dh)
    wexp = jnp.einsum('chjde,gf->chjgdfe', w1, eye).reshape(2, 2, st * gw, gw).astype(BF16)
    pos = jnp.broadcast_to(cmp_pos.reshape(2, 2, st, 1, dh), (2, 2, st, gg, dh)).reshape(2, 2, 1, st * gw).astype(F32)
    b1 = jnp.tile(cmp_b1, (1, gg)).reshape(2, 1, gw).astype(F32)
    w2 = jnp.einsum('cde,gf->cgdfe', cmp_w2, eye).reshape(2, gw, gw).astype(BF16)
    full = lambda shape: _const_spec(shape)
    return pl.pallas_call(
        functools.partial(_nsa_compress_body, n_cmp=n_cmp),
        grid=(b,),
        in_specs=[pl.BlockSpec((2, None, nr, st * gw), lambda i: (0, i, 0, 0)),
                  full((2, st * gw, gw)), full((2, st * gw, gw)), full((2, 1, st * gw)), full((2, 1, st * gw)),
                  full((2, 1, gw)), full((2, gw, gw))],
        out_specs=pl.BlockSpec((2, None, gg, nr, dh), lambda i: (0, i, 0, 0, 0)),
        out_shape=jax.ShapeDtypeStruct((2, b, gg, nr, dh), BF16),
        scratch_shapes=[pltpu.VMEM((nr + SUBLANES, gw), F32)],
        compiler_params=_cp("parallel"),
        name="nsa_compress",
    )(kv, wexp[:, 0], wexp[:, 1], pos[:, 0], pos[:, 1], b1, w2)


def _stacked_row_index(n_rep, tq):
    assert tq & (tq - 1) == 0
    return jnp.bitwise_and(lax.broadcasted_iota(jnp.int32, (n_rep * tq, 1), 0), tq - 1)


def _nt_dot(a, b):
    return lax.dot_general(a, b, (((1,), (1,)), ((), ())), preferred_element_type=F32)


def _nsa_cmp_select_body(q_ref, kc_ref, vc_ref, ovl_ref, oc_ref, ns_ref, sc_sc):
    tq = q_ref.shape[0]
    nr = kc_ref.shape[0]
    dh = NSA_HEAD_DIM
    nb = NSA_MAX_SEL_BLOCKS
    q0 = pl.program_id(2) * tq
    kc, vc = kc_ref[...], vc_ref[...]
    t_row = q0 + lax.broadcasted_iota(jnp.int32, (tq, 1), 0)
    t_lane = q0 + lax.broadcasted_iota(jnp.int32, (1, tq), 1)
    end_lane = lax.broadcasted_iota(jnp.int32, (1, nr), 1) * NSA_CMP_STRIDE + (NSA_CMP_BLOCK - 1)
    end_row = lax.broadcasted_iota(jnp.int32, (nr, 1), 0) * NSA_CMP_STRIDE + (NSA_CMP_BLOCK - 1)
    mask = end_lane <= t_row
    mask_t = end_row <= t_lane
    psum_t = jnp.zeros((nr, tq), F32)
    outs = []
    for h in range(NSA_REP):
        qh = q_ref[:, h * dh:(h + 1) * dh]
        s = jnp.where(mask, _nt_dot(qh, kc), NEG_INF)
        e = jnp.where(mask, jnp.exp(s - jnp.max(s, axis=-1, keepdims=True)), 0.0)
        den = jnp.sum(e, axis=-1, keepdims=True)
        p = e / jnp.where(den > 0.0, den, 1.0)
        outs.append(_dot(p.astype(BF16), vc))
        s_t = jnp.where(mask_t, _nt_dot(kc, qh), NEG_INF)
        e_t = jnp.where(mask_t, jnp.exp(s_t - jnp.max(s_t, axis=0, keepdims=True)), 0.0)
        den_t = jnp.sum(e_t, axis=0, keepdims=True)
        psum_t = psum_t + e_t / jnp.where(den_t > 0.0, den_t, 1.0)
    oc_ref[...] = jnp.concatenate(outs, axis=1).astype(BF16)

    p_hi = psum_t.astype(BF16)
    p_lo = (psum_t - p_hi.astype(F32)).astype(BF16)
    imp = _dot(ovl_ref[...], p_hi) + _dot(ovl_ref[...], p_lo)
    blk = lax.broadcasted_iota(jnp.int32, (nb, 1), 0)
    cur = t_lane // NSA_SLC_BLOCK
    valid = blk <= cur
    forced = (blk == 0) | (blk >= cur - (NSA_LOCAL_BLOCKS - 1))
    score = jnp.where(valid, jnp.where(forced, BIG, imp), -BIG)
    sc_sc[...] = score
    n_tiles = nb // SUBLANES
    tiles = [score[a * SUBLANES:(a + 1) * SUBLANES] for a in range(n_tiles)]
    ranks = [jnp.zeros((SUBLANES, tq), F32) for _ in range(n_tiles)]
    sub = lax.broadcasted_iota(jnp.int32, (SUBLANES, 1), 0)
    for j in range(nb):
        sj = sc_sc[j:j + 1, :]
        for a in range(n_tiles):
            ge = jnp.where(sj >= tiles[a], 1.0, 0.0)
            gt = jnp.where(sj > tiles[a], 1.0, 0.0)
            if a * SUBLANES > j:
                ranks[a] = ranks[a] + ge
            elif a * SUBLANES + SUBLANES - 1 < j:
                ranks[a] = ranks[a] + gt
            else:
                ranks[a] = ranks[a] + jnp.where(sub > j - a * SUBLANES, ge, gt)
    rank = jnp.concatenate(ranks, axis=0)
    notsel = jnp.where(rank < float(NSA_TOP_N), 0.0, 1.0)
    notsel = jnp.concatenate([notsel, jnp.ones((LANES - nb, tq), F32)], axis=0)
    ns_ref[...] = notsel.T[:, :nb].astype(BF16)


def _nsa_cmp_select(q, kvc, tq=256):
    b, s, qd = q.shape
    gg, dh = NSA_KV_GROUPS, NSA_HEAD_DIM
    nr = kvc.shape[3]
    nb = NSA_MAX_SEL_BLOCKS
    tq = min(tq, s)
    assert s // NSA_SLC_BLOCK <= nb
    c_start = np.arange(nr) * NSA_CMP_STRIDE
    c_end = c_start + NSA_CMP_BLOCK - 1
    s_start = np.arange(nb) * NSA_SLC_BLOCK
    s_end = s_start + NSA_SLC_BLOCK - 1
    ovl = ((c_start[None, :] <= s_end[:, None]) & (c_end[None, :] >= s_start[:, None])).astype(np.float32)
    return pl.pallas_call(
        _nsa_cmp_select_body,
        grid=(b, gg, s // tq),
        in_specs=[pl.BlockSpec((None, tq, NSA_REP * dh), lambda i, g, j: (i, j, g)),
                  pl.BlockSpec((None, None, None, nr, dh), lambda i, g, j: (0, i, g, 0, 0)),
                  pl.BlockSpec((None, None, None, nr, dh), lambda i, g, j: (1, i, g, 0, 0)),
                  _const_spec((nb, nr))],
        out_specs=[pl.BlockSpec((None, tq, NSA_REP * dh), lambda i, g, j: (i, j, g)),
                   pl.BlockSpec((None, None, tq, nb), lambda i, g, j: (i, g, j, 0))],
        out_shape=[jax.ShapeDtypeStruct((b, s, qd), BF16), jax.ShapeDtypeStruct((b, gg, s, nb), BF16)],
        scratch_shapes=[pltpu.VMEM((nb, tq), F32)],
        compiler_params=_cp("parallel", "parallel", "parallel"),
        name="nsa_cmp_select",
    )(q, kvc, kvc, jnp.asarray(ovl, BF16))


def _nsa_sel_body(qi_ref, ki_ref, q_ref, ns_ref, k_ref, v_ref, o_ref, qa_sc, m_sc, l_sc, acc_sc):
    tq = q_ref.shape[0]
    tk = k_ref.shape[0]
    dh = NSA_HEAD_DIM
    p_idx = pl.program_id(2)
    qi, ki = qi_ref[p_idx], ki_ref[p_idx]

    @pl.when(ki == 0)
    def _():
        for h in range(NSA_REP):
            qa_sc[h * tq:(h + 1) * tq, :] = jnp.concatenate([q_ref[:, h * dh:(h + 1) * dh], ns_ref[...]], axis=1)
        m_sc[...] = jnp.full(m_sc.shape, NEG_INF, F32)
        l_sc[...] = jnp.zeros(l_sc.shape, F32)
        acc_sc[...] = jnp.zeros(acc_sc.shape, F32)

    s = _nt_dot(qa_sc[...], k_ref[...])
    t_pos = qi * tq + _stacked_row_index(NSA_REP, tq)
    k_pos = ki * tk + lax.broadcasted_iota(jnp.int32, (1, tk), 1)
    s = jnp.where(k_pos <= t_pos, s, NEG_INF)
    m_old = m_sc[...]
    m_new = jnp.maximum(m_old, jnp.max(s, axis=-1, keepdims=True))
    alpha = jnp.exp(m_old - m_new)
    p = jnp.exp(s - m_new)
    l_sc[...] = alpha * l_sc[...] + jnp.sum(p, axis=-1, keepdims=True)
    acc_sc[...] = alpha * acc_sc[...] + _dot(p.astype(BF16), v_ref[...])
    m_sc[...] = m_new

    @pl.when(ki == (qi * tq + tq - 1) // tk)
    def _():
        o = acc_sc[...] / l_sc[...]
        o_ref[...] = jnp.concatenate([o[h * tq:(h + 1) * tq] for h in range(NSA_REP)], axis=1).astype(BF16)


def _nsa_selected(q, notsel, ksa, vs, tq=256, tk=512):
    b, s, qd = q.shape
    gg, dh, rr = NSA_KV_GROUPS, NSA_HEAD_DIM, NSA_REP
    tq, tk = min(tq, s), min(tk, s)
    pairs = [(qi, ki) for qi in range(s // tq) for ki in range((qi * tq + tq - 1) // tk + 1)]
    qi_arr = jnp.asarray([p[0] for p in pairs], jnp.int32)
    ki_arr = jnp.asarray([p[1] for p in pairs], jnp.int32)
    grid_spec = pltpu.PrefetchScalarGridSpec(
        num_scalar_prefetch=2,
        grid=(b, gg, len(pairs)),
        in_specs=[pl.BlockSpec((None, tq, rr * dh), lambda i, g, p, qi, ki: (i, qi[p], g)),
                  pl.BlockSpec((None, None, tq, NSA_MAX_SEL_BLOCKS), lambda i, g, p, qi, ki: (i, g, qi[p], 0)),
                  pl.BlockSpec((None, tk, LANES), lambda i, g, p, qi, ki: (i, ki[p], g)),
                  pl.BlockSpec((None, None, tk, dh), lambda i, g, p, qi, ki: (i, g, ki[p], 0))],
        out_specs=pl.BlockSpec((None, tq, rr * dh), lambda i, g, p, qi, ki: (i, qi[p], g)),
        scratch_shapes=[pltpu.VMEM((rr * tq, LANES), BF16), pltpu.VMEM((rr * tq, 1), F32),
                        pltpu.VMEM((rr * tq, 1), F32), pltpu.VMEM((rr * tq, dh), F32)])
    return pl.pallas_call(
        _nsa_sel_body,
        grid_spec=grid_spec,
        out_shape=jax.ShapeDtypeStruct((b, s, qd), BF16),
        compiler_params=_cp("parallel", "parallel", "arbitrary"),
        name="nsa_selected",
    )(qi_arr, ki_arr, q, notsel, ksa, vs)


def _nsa_window_body(q_ref, *refs, n_kv):
    k_refs, v_refs, o_ref = refs[:n_kv], refs[n_kv:2 * n_kv], refs[2 * n_kv]
    tq = q_ref.shape[0]
    dh = NSA_HEAD_DIM
    qi = pl.program_id(2)
    qs = jnp.concatenate([q_ref[:, h * dh:(h + 1) * dh] for h in range(NSA_REP)], axis=0)
    t_pos = qi * tq + _stacked_row_index(NSA_REP, tq)
    parts = []
    for j in range(n_kv):
        k_pos = (qi - (n_kv - 1) + j) * tq + lax.broadcasted_iota(jnp.int32, (1, tq), 1)
        rel = t_pos - k_pos
        ok = (rel >= 0) & (rel < NSA_WINDOW) & (k_pos >= 0)
        parts.append(jnp.where(ok, _nt_dot(qs, k_refs[j][...]), NEG_INF))
    s = jnp.concatenate(parts, axis=1)
    p = jnp.exp(s - jnp.max(s, axis=-1, keepdims=True))
    p = p / jnp.sum(p, axis=-1, keepdims=True)
    o = jnp.zeros((NSA_REP * tq, dh), F32)
    for j in range(n_kv):
        o = o + _dot(p[:, j * tq:(j + 1) * tq].astype(BF16), v_refs[j][...])
    o_ref[...] = jnp.concatenate([o[h * tq:(h + 1) * tq] for h in range(NSA_REP)], axis=1).astype(BF16)


def _nsa_window(q, kw, vw, tq=256):
    b, s, qd = q.shape
    gg, dh, rr = NSA_KV_GROUPS, NSA_HEAD_DIM, NSA_REP
    tq = min(tq, s)
    assert NSA_WINDOW % tq == 0
    n_kv = NSA_WINDOW // tq + 1
    kv_spec = lambda off: pl.BlockSpec((None, None, tq, dh), lambda i, g, j: (i, g, jnp.maximum(j - off, 0), 0))
    kv_specs = [kv_spec(n_kv - 1 - j) for j in range(n_kv)]
    return pl.pallas_call(
        functools.partial(_nsa_window_body, n_kv=n_kv),
        grid=(b, gg, s // tq),
        in_specs=[pl.BlockSpec((None, tq, rr * dh), lambda i, g, j: (i, j, g))] + kv_specs + kv_specs,
        out_specs=pl.BlockSpec((None, tq, rr * dh), lambda i, g, j: (i, j, g)),
        out_shape=jax.ShapeDtypeStruct((b, s, qd), BF16),
        compiler_params=_cp("parallel", "parallel", "parallel"),
        name="nsa_window",
    )(q, *([kw] * n_kv), *([vw] * n_kv))


def _nsa_out_body(x_ref, oc_ref, os_ref, ow_ref, gt_ref, ex_ref, w_ref, o_ref):
    qd = oc_ref.shape[1]
    gt = gt_ref[...]
    g_hi = gt.astype(BF16)
    g_lo = (gt - g_hi.astype(F32)).astype(BF16)
    o = jnp.zeros(oc_ref.shape, F32)
    for br, ref in enumerate((oc_ref, os_ref, ow_ref)):
        ex = ex_ref[:, br * qd:(br + 1) * qd]
        o = o + (_dot(g_hi, ex) + _dot(g_lo, ex)) * ref[...].astype(F32)
    o_ref[...] = x_ref[...] + _dot(o.astype(BF16), w_ref[...])


def _nsa_out(x2, oc, osel, ow, gates, w_out, tm=512):
    t, d = x2.shape
    qd = oc.shape[1]
    tm = min(tm, t)
    col = np.arange(3 * qd)
    expand = (np.arange(LANES)[:, None] == (col // qd) * NSA_HEADS + (col % qd) // NSA_HEAD_DIM).astype(np.float32)
    tok = lambda w: pl.BlockSpec((tm, w), lambda i: (i, 0))
    return pl.pallas_call(
        _nsa_out_body,
        grid=(t // tm,),
        in_specs=[tok(d), tok(qd), tok(qd), tok(qd), tok(LANES), _const_spec((LANES, 3 * qd)), _const_spec((qd, d))],
        out_specs=tok(d),
        out_shape=jax.ShapeDtypeStruct((t, d), F32),
        compiler_params=_cp("parallel"),
        name="nsa_out",
    )(x2, oc, osel, ow, gates, jnp.asarray(expand, BF16), w_out.astype(BF16))


def _nsa_layer(x3, positions, g_mix, w_in, q_gain, k_gain, cmp_pos, cmp_w1, cmp_b1, cmp_w2, w_out):
    b, s, d = x3.shape
    rope = _rope_tables(positions)
    q, kc, kw, ksa, vc, vs, vw, gates = _nsa_proj(x3, g_mix, w_in, q_gain, k_gain, rope)
    kvc = _nsa_compress(kc, vc, cmp_pos, cmp_w1, cmp_b1, cmp_w2)
    oc, notsel = _nsa_cmp_select(q, kvc)
    osel = _nsa_selected(q, notsel, ksa, vs)
    ow = _nsa_window(q, kw, vw)
    t = b * s
    flat = lambda a: a.reshape(t, a.shape[-1])
    return _nsa_out(x3.reshape(t, d), flat(oc), flat(osel), flat(ow), flat(gates), w_out).reshape(b, s, d)


def kernel(x, positions, norm_mix, norm_ffn, ffn_w_in, ffn_w_out, conv_w_in, conv_b_in, conv_w_dw, conv_b_dw, conv_ln_g, conv_ln_b, conv_w_out, nsa_w_in, nsa_q_gain, nsa_k_gain, nsa_cmp_pos, nsa_cmp_w1, nsa_cmp_b1, nsa_cmp_w2, nsa_w_out, s5_lam_re, s5_lam_im, s5_log_step, s5_b_re, s5_b_im, s5_c_re, s5_c_im, s5_d, s5_w_glu, s5_b_glu, pool_w, pool_scale):
    b, s, d = x.shape
    depth = norm_mix.shape[0]
    for i in range(depth):
        m, j = i % N_MIXERS, i // N_MIXERS
        if m == 0:
            x = _conv_layer(x, norm_mix[i], conv_w_in[j], conv_b_in[j], conv_w_dw[j], conv_b_dw[j],
                            conv_ln_g[j], conv_ln_b[j], conv_w_out[j])
        elif m == 1:
            x = _nsa_layer(x, positions, norm_mix[i], nsa_w_in[j], nsa_q_gain[j], nsa_k_gain[j], nsa_cmp_pos[j],
                           nsa_cmp_w1[j], nsa_cmp_b1[j], nsa_cmp_w2[j], nsa_w_out[j])
        elif m == 2:
            x = _s5_layer(x, norm_mix[i], s5_lam_re[j], s5_lam_im[j], s5_log_step[j], s5_b_re[j], s5_b_im[j],
                          s5_c_re[j], s5_c_im[j], s5_d[j], s5_w_glu[j], s5_b_glu[j])
        elif m == 3:
            x = _pool_layer(x, norm_mix[i], pool_w[j], pool_scale[j])
        x = _ffn(x.reshape(b * s, d), norm_ffn[i], ffn_w_in[i], ffn_w_out[i]).reshape(b, s, d)
    return x
```

```python
import functools
import math

import numpy as np
import jax
import jax.numpy as jnp
from jax import lax
from jax.experimental import pallas as pl
from jax.experimental.pallas import tpu as pltpu

F32 = jnp.float32
BF16 = jnp.bfloat16

RMS_EPS = 1e-6
LN_EPS = 1e-5
N_MIXERS = 4
CONV_WIDTH = 31
POOL_WINDOWS = (2, 4, 8, 16)

V7X_VMEM_LIMIT_BYTES = 56 * 1024 * 1024
SUBLANES = 8
LANES = 128


def _cp(*sem):
    return pltpu.CompilerParams(dimension_semantics=sem, vmem_limit_bytes=V7X_VMEM_LIMIT_BYTES)


def _const_spec(shape):
    nd = len(shape)
    return pl.BlockSpec(shape, lambda *_: (0,) * nd, pipeline_mode=pl.Buffered(1))


def _dot(a, b):
    return jnp.dot(a, b, preferred_element_type=F32)


def _rms(x, g):
    ms = jnp.mean(x * x, axis=-1, keepdims=True)
    return x * lax.rsqrt(ms + RMS_EPS) * g


def _row(v):
    return v.reshape(1, -1).astype(F32)


FFN_CHUNK = 256


def _ffn_body(x_ref, g_ref, wg_ref, wu_ref, wo_ref, o_ref, act_sc):
    x = x_ref[...]
    h = _rms(x, g_ref[...]).astype(BF16)
    n_chunks = act_sc.shape[1] // FFN_CHUNK
    for c in range(n_chunks):
        sl = slice(c * FFN_CHUNK, (c + 1) * FFN_CHUNK)
        gate = _dot(h, wg_ref[:, sl])
        up = _dot(h, wu_ref[:, sl])
        act_sc[:, sl] = (gate * jax.nn.sigmoid(gate) * up).astype(BF16)
    o_ref[...] = x + _dot(act_sc[...], wo_ref[...])


def _ffn(x2, g, w_in, w_out, tm=512):
    t, d = x2.shape
    f = w_out.shape[0]
    w_in = w_in.astype(BF16)
    w_out = w_out.astype(BF16)
    return pl.pallas_call(
        _ffn_body,
        grid=(t // tm,),
        in_specs=[
            pl.BlockSpec((tm, d), lambda i: (i, 0)),
            _const_spec((1, d)),
            pl.BlockSpec((d, f), lambda i: (0, 0), pipeline_mode=pl.Buffered(1)),
            pl.BlockSpec((d, f), lambda i: (0, 1), pipeline_mode=pl.Buffered(1)),
            _const_spec((f, d)),
        ],
        out_specs=pl.BlockSpec((tm, d), lambda i: (i, 0)),
        out_shape=jax.ShapeDtypeStruct((t, d), F32),
        scratch_shapes=[pltpu.VMEM((tm, f), BF16)],
        compiler_params=_cp("parallel"),
        name="ffn",
    )(x2, _row(g), w_in, w_in, w_out)


CONV_HALO = 32
CONV_ROWS = 64


def _conv_body(x_ref, g_ref, wa_ref, wg_ref, ba_ref, bg_ref, wdw_ref, bdw_ref,
               lng_ref, lnb_ref, wo_ref, o_ref, u_sc, v_sc):
    tm, d = x_ref.shape
    s = pl.program_id(1)

    @pl.when(s == 0)
    def _():
        u_sc[0:CONV_HALO, :] = jnp.zeros((CONV_HALO, d), F32)

    @pl.when(s > 0)
    def _():
        u_sc[0:CONV_HALO, :] = u_sc[tm:tm + CONV_HALO, :]

    x = x_ref[...]
    h = _rms(x, g_ref[...]).astype(BF16)
    a = _dot(h, wa_ref[...]) + ba_ref[...]
    gt = _dot(h, wg_ref[...]) + bg_ref[...]
    u_sc[CONV_HALO:CONV_HALO + tm, :] = a * jax.nn.sigmoid(gt)

    off = CONV_HALO - (CONV_WIDTH - 1)

    def conv_rows(c, carry):
        r0 = pl.multiple_of(c * CONV_ROWS, CONV_ROWS)
        for lt in range(d // LANES):
            lanes = slice(lt * LANES, (lt + 1) * LANES)
            win = u_sc[pl.ds(r0, CONV_ROWS + CONV_HALO), lanes]
            acc = jnp.zeros((CONV_ROWS, LANES), F32) + bdw_ref[:, lanes]
            for sft in range(SUBLANES):
                shifted = pltpu.roll(win, CONV_ROWS + CONV_HALO - sft, axis=0) if sft else win
                for j in range(sft, CONV_HALO + 1, SUBLANES):
                    k = j - off
                    if 0 <= k < CONV_WIDTH:
                        acc = acc + wdw_ref[k:k + 1, lanes] * shifted[j - sft:j - sft + CONV_ROWS]
            v_sc[pl.ds(r0, CONV_ROWS), lanes] = acc
        return carry

    lax.fori_loop(0, tm // CONV_ROWS, conv_rows, 0)

    v = v_sc[...]
    mu = jnp.mean(v, axis=-1, keepdims=True)
    vc = v - mu
    var = jnp.mean(vc * vc, axis=-1, keepdims=True)
    y = vc * lax.rsqrt(var + LN_EPS) * lng_ref[...] + lnb_ref[...]
    y = (y * jax.nn.sigmoid(y)).astype(BF16)
    o_ref[...] = x + _dot(y, wo_ref[...])


def _conv_layer(x3, g, w_in, b_in, w_dw, b_dw, ln_g, ln_b, w_out, tm=256):
    b, s, d = x3.shape
    w_in = w_in.astype(BF16)
    return pl.pallas_call(
        _conv_body,
        grid=(b, s // tm),
        in_specs=[
            pl.BlockSpec((None, tm, d), lambda i, j: (i, j, 0)),
            _const_spec((1, d)),
            pl.BlockSpec((d, d), lambda i, j: (0, 0), pipeline_mode=pl.Buffered(1)),
            pl.BlockSpec((d, d), lambda i, j: (0, 1), pipeline_mode=pl.Buffered(1)),
            pl.BlockSpec((1, d), lambda i, j: (0, 0), pipeline_mode=pl.Buffered(1)),
            pl.BlockSpec((1, d), lambda i, j: (0, 1), pipeline_mode=pl.Buffered(1)),
            _const_spec((CONV_WIDTH, d)),
            _const_spec((1, d)),
            _const_spec((1, d)),
            _const_spec((1, d)),
            _const_spec((d, d)),
        ],
        out_specs=pl.BlockSpec((None, tm, d), lambda i, j: (i, j, 0)),
        out_shape=jax.ShapeDtypeStruct((b, s, d), F32),
        scratch_shapes=[pltpu.VMEM((CONV_HALO + tm, d), F32), pltpu.VMEM((tm, d), F32)],
        compiler_params=_cp("parallel", "arbitrary"),
        name="conv_module",
    )(x3, _row(g), w_in, w_in, _row(b_in), _row(b_in), w_dw.astype(F32), _row(b_dw),
      _row(ln_g), _row(ln_b), w_out.astype(BF16))


POOL_HALO = 16


def _pool_body(x_ref, g_ref, w_ref, sc_ref, o_ref, h_sc):
    tm, d = x_ref.shape
    s = pl.program_id(1)
    grp = d // len(POOL_WINDOWS)

    @pl.when(s == 0)
    def _():
        h_sc[0:POOL_HALO, :] = jnp.zeros((POOL_HALO, d), F32)

    @pl.when(s > 0)
    def _():
        h_sc[0:POOL_HALO, :] = h_sc[tm:tm + POOL_HALO, :]

    x = x_ref[...]
    h_sc[POOL_HALO:POOL_HALO + tm, :] = _rms(x, g_ref[...])
    t = s * tm + lax.broadcasted_iota(jnp.int32, (tm, 1), 0)
    for gi, win in enumerate(POOL_WINDOWS):
        lo = gi * grp
        hcur = h_sc[POOL_HALO:POOL_HALO + tm, lo:lo + grp]
        tot = hcur
        for k in range(1, win):
            tot = tot + h_sc[POOL_HALO - k:POOL_HALO - k + tm, lo:lo + grp]
        cnt = jnp.minimum(t + 1, win).astype(F32)
        pooled = (tot * (1.0 / cnt) - hcur).astype(BF16)
        y = _dot(pooled, w_ref[gi]) * sc_ref[:, lo:lo + grp]
        o_ref[:, lo:lo + grp] = x[:, lo:lo + grp] + y


def _pool_layer(x3, g, w_grp, scale, tm=512):
    b, s, d = x3.shape
    ng, grp, _ = w_grp.shape
    return pl.pallas_call(
        _pool_body,
        grid=(b, s // tm),
        in_specs=[
            pl.BlockSpec((None, tm, d), lambda i, j: (i, j, 0)),
            _const_spec((1, d)),
            _const_spec((ng, grp, grp)),
            _const_spec((1, d)),
        ],
        out_specs=pl.BlockSpec((None, tm, d), lambda i, j: (i, j, 0)),
        out_shape=jax.ShapeDtypeStruct((b, s, d), F32),
        scratch_shapes=[pltpu.VMEM((POOL_HALO + tm, d), F32)],
        compiler_params=_cp("parallel", "arbitrary"),
        name="pool",
    )(x3, _row(g), w_grp.astype(BF16), _row(scale))


S5_L = 8
S5_GPT = 8


def _s5_operators(lam_re, lam_im, log_step, b_re, b_im, c_re, c_im):
    hi = lax.Precision.HIGHEST
    n_g, n_p = lam_re.shape
    n_c = b_re.shape[-1]
    n_j = n_g // S5_GPT
    L = S5_L
    def cmul(a, b):
        return a[0] * b[0] - a[1] * b[1], a[0] * b[1] + a[1] * b[0]

    def cexp(z):
        mag = jnp.exp(z[0])
        return mag * jnp.cos(z[1]), mag * jnp.sin(z[1])

    lr, li = lam_re.astype(F32), lam_im.astype(F32)
    step = jnp.exp(log_step.astype(F32))[:, None]
    lam_bar = cexp((lr * step, li * step))
    inv_den = 1.0 / (lr * lr + li * li)
    zoh = cmul((lam_bar[0] - 1.0, lam_bar[1]), (lr * inv_den, -li * inv_den))
    b_bar = cmul((zoh[0][:, :, None], zoh[1][:, :, None]), (b_re.astype(F32), b_im.astype(F32)))
    c_mat = (c_re.astype(F32), c_im.astype(F32))
    dd = jnp.arange(2 * L, dtype=F32)[:, None, None]
    pw = cexp(((lr * step)[None] * dd, (li * step)[None] * dd))
    eye = jnp.eye(S5_GPT, dtype=F32)

    pb = cmul((pw[0][:L, :, :, None], pw[1][:L, :, :, None]), (b_bar[0][None], b_bar[1][None]))
    kern = (jnp.einsum('gap,dgpc->gdac', c_mat[0], pb[0], precision=hi)
            - jnp.einsum('gap,dgpc->gdac', c_mat[1], pb[1], precision=hi))
    sig = np.arange(L)[:, None]
    tau = np.arange(L)[None, :]
    lag = np.clip(tau - sig, 0, L - 1)
    kt = kern[:, lag] * jnp.asarray((tau >= sig).astype(np.float32))[None, :, :, None, None]
    kt = kt.reshape(n_j, S5_GPT, L, L, n_c, n_c)
    m_op = jnp.einsum('jgstac,gh->jsgctha', kt, eye).reshape(n_j, L * S5_GPT * n_c, L * S5_GPT * n_c)

    e = jnp.stack(pb, axis=0)[:, ::-1].transpose(0, 2, 1, 3, 4)
    e = e.reshape(2, n_j, S5_GPT, L, n_p, n_c)
    b_op = jnp.einsum('rjgspc,gh->jsgcrhp', e, eye).reshape(n_j, L * S5_GPT * n_c, 2 * S5_GPT * n_p)

    f = cmul((c_mat[0][:, None], c_mat[1][:, None]),
             (pw[0][1:L + 1].transpose(1, 0, 2)[:, :, None, :], pw[1][1:L + 1].transpose(1, 0, 2)[:, :, None, :]))
    f = jnp.stack([f[0], -f[1]], axis=0).reshape(2, n_j, S5_GPT, L, n_c, n_p)
    c_op = jnp.einsum('rjgtap,gh->jrgptha', f, eye).reshape(n_j, 2 * S5_GPT * n_p, L * S5_GPT * n_c)

    ak = cexp(((lr * step * L)[None] * dd, (li * step * L)[None] * dd))
    ak = jnp.stack(ak, axis=1).reshape(2 * L, 2, n_j, S5_GPT * n_p)
    a_pw = ak.transpose(2, 0, 1, 3).reshape(n_j, 2 * L, 2 * S5_GPT * n_p)
    return m_op.astype(BF16), b_op.astype(BF16), c_op.astype(BF16), a_pw.astype(F32)


def _s5_regroup_body(x_ref, g_ref, o_ref, u_sc):
    tm, d = x_ref.shape
    u = _rms(x_ref[...], g_ref[...])
    rows = tm // S5_L
    for j in range(d // LANES):
        u_sc[j] = u[:, j * LANES:(j + 1) * LANES]
    for j in range(d // LANES):
        for sg in range(S5_L):
            o_ref[j, :, sg * LANES:(sg + 1) * LANES] = (
                u_sc[j, pl.ds(sg, rows, stride=S5_L), :].astype(BF16))


def _s5_regroup(x2, g, tm=512):
    t, d = x2.shape
    n_j = d // LANES
    return pl.pallas_call(
        _s5_regroup_body,
        grid=(t // tm,),
        in_specs=[pl.BlockSpec((tm, d), lambda i: (i, 0)), _const_spec((1, d))],
        out_specs=pl.BlockSpec((n_j, tm // S5_L, S5_L * LANES), lambda i: (0, i, 0)),
        out_shape=jax.ShapeDtypeStruct((n_j, t // S5_L, S5_L * LANES), BF16),
        scratch_shapes=[pltpu.VMEM((n_j, tm, LANES), F32)],
        compiler_params=_cp("parallel"),
        name="s5_regroup",
    )(x2, _row(g))


def _s5_chunk_body(u_ref, m_ref, b_ref, c_ref, a_ref, y_ref, xr_sc, xi_sc, pr_sc, pi_sc, cr_sc, ci_sc):
    rows, width = u_ref.shape
    half = width // 2
    nt = pl.program_id(2)

    @pl.when(nt == 0)
    def _():
        cr_sc[...] = jnp.zeros_like(cr_sc)
        ci_sc[...] = jnp.zeros_like(ci_sc)

    u = u_ref[...]
    xin = _dot(u, b_ref[...])
    xr, xi = xin[:, :half], xin[:, half:]
    row = lax.broadcasted_iota(jnp.int32, (rows, 1), 0) % SUBLANES
    for shift in (1, 2, 4):
        ar, ai = a_ref[shift:shift + 1, :half], a_ref[shift:shift + 1, half:]
        keep = row >= shift
        sr = jnp.where(keep, pltpu.roll(xr, shift, axis=0), 0.0)
        si = jnp.where(keep, pltpu.roll(xi, shift, axis=0), 0.0)
        xr, xi = xr + ar * sr - ai * si, xi + ar * si + ai * sr
    xr_sc[...] = xr
    xi_sc[...] = xi
    keep = row >= 1
    pr_sc[...] = jnp.where(keep, pltpu.roll(xr, 1, axis=0), 0.0)
    pi_sc[...] = jnp.where(keep, pltpu.roll(xi, 1, axis=0), 0.0)

    akr, aki = a_ref[0:SUBLANES, :half], a_ref[0:SUBLANES, half:]
    a8r, a8i = a_ref[SUBLANES:SUBLANES + 1, :half], a_ref[SUBLANES:SUBLANES + 1, half:]

    def tile_step(t, carry):
        cr, ci = carry
        base = pl.multiple_of(t * SUBLANES, SUBLANES)
        pr_sc[pl.ds(base, SUBLANES), :] = pr_sc[pl.ds(base, SUBLANES), :] + akr * cr - aki * ci
        pi_sc[pl.ds(base, SUBLANES), :] = pi_sc[pl.ds(base, SUBLANES), :] + akr * ci + aki * cr
        lr = xr_sc[pl.ds(base + SUBLANES - 1, 1), :]
        li = xi_sc[pl.ds(base + SUBLANES - 1, 1), :]
        return lr + a8r * cr - a8i * ci, li + a8r * ci + a8i * cr

    cr, ci = lax.fori_loop(0, rows // SUBLANES, tile_step, (cr_sc[...], ci_sc[...]))
    cr_sc[...] = cr
    ci_sc[...] = ci

    y = _dot(u, m_ref[...])
    y = y + _dot(pr_sc[...].astype(BF16), c_ref[0:half, :])
    y = y + _dot(pi_sc[...].astype(BF16), c_ref[half:, :])
    y_ref[...] = y


def _s5_chunks(u3, m_op, b_op, c_op, a_pw, bsz):
    n_j, n_rows, width = u3.shape
    rows_per_seq = n_rows // bsz
    rows = min(512, rows_per_seq)
    nt = rows_per_seq // rows
    half = width // 2
    wspec = lambda shape: pl.BlockSpec((None,) + shape, lambda j, b, t: (j, 0, 0), pipeline_mode=pl.Buffered(1))
    return pl.pallas_call(
        _s5_chunk_body,
        grid=(n_j, bsz, nt),
        in_specs=[
            pl.BlockSpec((None, rows, width), lambda j, b, t: (j, b * nt + t, 0)),
            wspec((width, width)), wspec((width, width)), wspec((width, width)),
            wspec((2 * S5_L, width)),
        ],
        out_specs=pl.BlockSpec((None, rows, width), lambda j, b, t: (j, b * nt + t, 0)),
        out_shape=jax.ShapeDtypeStruct((n_j, n_rows, width), F32),
        scratch_shapes=[pltpu.VMEM((rows, half), F32)] * 4 + [pltpu.VMEM((1, half), F32)] * 2,
        compiler_params=_cp("arbitrary", "arbitrary", "arbitrary"),
        name="s5_chunks",
    )(u3, m_op, b_op, c_op, a_pw)


def _s5_out_body(x_ref, y3_ref, g_ref, d_ref, w_ref, b_ref, o_ref, y_sc):
    tm, d = x_ref.shape
    rows = tm // S5_L
    for j in range(d // LANES):
        for tau in range(S5_L):
            y_sc[j, pl.ds(tau, rows, stride=S5_L), :] = y3_ref[j, :, tau * LANES:(tau + 1) * LANES]
    x = x_ref[...]
    u = _rms(x, g_ref[...])
    y_ssm = jnp.concatenate([y_sc[j] for j in range(d // LANES)], axis=1)
    y = jax.nn.gelu(y_ssm + d_ref[...] * u).astype(BF16)
    z = _dot(y, w_ref[...]) + b_ref[...]
    o_ref[...] = x + z[:, :d] * jax.nn.sigmoid(z[:, d:])


def _s5_out(x2, y3, g, d_skip, w_glu, b_glu, tm=512):
    t, d = x2.shape
    n_j = d // LANES
    return pl.pallas_call(
        _s5_out_body,
        grid=(t // tm,),
        in_specs=[
            pl.BlockSpec((tm, d), lambda i: (i, 0)),
            pl.BlockSpec((n_j, tm // S5_L, S5_L * LANES), lambda i: (0, i, 0)),
            _const_spec((1, d)), _const_spec((1, d)),
            _const_spec((d, 2 * d)), _const_spec((1, 2 * d)),
        ],
        out_specs=pl.BlockSpec((tm, d), lambda i: (i, 0)),
        out_shape=jax.ShapeDtypeStruct((t, d), F32),
        scratch_shapes=[pltpu.VMEM((n_j, tm, LANES), F32)],
        compiler_params=_cp("parallel"),
        name="s5_out",
    )(x2, y3, _row(g), _row(d_skip), w_glu.astype(BF16), _row(b_glu))


def _s5_layer(x3, g, lam_re, lam_im, log_step, b_re, b_im, c_re, c_im, d_skip, w_glu, b_glu):
    b, s, d = x3.shape
    x2 = x3.reshape(b * s, d)
    m_op, b_op, c_op, a_pw = _s5_operators(lam_re, lam_im, log_step, b_re, b_im, c_re, c_im)
    u3 = _s5_regroup(x2, g)
    y3 = _s5_chunks(u3, m_op, b_op, c_op, a_pw, b)
    return _s5_out(x2, y3, g, d_skip, w_glu, b_glu).reshape(b, s, d)


NSA_HEAD_DIM = 64
NSA_HEADS = 16
NSA_KV_GROUPS = 4
NSA_REP = NSA_HEADS // NSA_KV_GROUPS
NSA_CMP_BLOCK = 32
NSA_CMP_STRIDE = 16
NSA_SLC_BLOCK = 64
NSA_TOP_N = 16
NSA_LOCAL_BLOCKS = 2
NSA_WINDOW = 512
ROPE_THETA = 500000.0
ROPE_DIMS = NSA_HEAD_DIM // 4
NEG_INF = -1e30
BIG = 1e9
LOG2_E = 1.4426950408889634
NSA_MAX_SEL_BLOCKS = 64
NSA_NORM_SECTIONS = 8


def _rope_body(pos_ref, inv_ref, c_ref, s1_ref, s2_ref):
    ang = pos_ref[...].astype(F32) * inv_ref[...]
    lane = lax.broadcasted_iota(jnp.int32, ang.shape, 1) % NSA_HEAD_DIM
    cos, sin = jnp.cos(ang), jnp.sin(ang)
    half = ROPE_DIMS // 2
    c_ref[...] = jnp.where(lane < ROPE_DIMS, cos, 1.0)
    s1_ref[...] = jnp.where(lane < half, -sin, 0.0)
    s2_ref[...] = jnp.where(lane < half, 0.0, jnp.where(lane < ROPE_DIMS, sin, 0.0))


def _rope_tables(positions):
    s = positions.shape[0]
    half = ROPE_DIMS // 2
    inv_freq = ROPE_THETA ** (-jnp.arange(half, dtype=F32) / half)
    lane = np.arange(LANES) % NSA_HEAD_DIM
    inv_lane = jnp.where(jnp.asarray(lane < ROPE_DIMS), inv_freq[lane % half], 0.0).reshape(1, LANES)
    ts = min(s, 512)
    spec = pl.BlockSpec((ts, LANES), lambda i: (i, 0))
    return pl.pallas_call(
        _rope_body,
        grid=(s // ts,),
        in_specs=[pl.BlockSpec((ts, 1), lambda i: (i, 0)), _const_spec((1, LANES))],
        out_specs=[spec, spec, spec],
        out_shape=[jax.ShapeDtypeStruct((s, LANES), F32)] * 3,
        compiler_params=_cp("parallel"),
        name="rope_tables",
    )(positions.reshape(s, 1), inv_lane)


def _nsa_proj_body(x_ref, g_ref, wn_ref, wv_ref, wg_ref, gain_ref, bd_ref, c_ref, s1_ref, s2_ref,
                   q_ref, kc_ref, kw_ref, ksa_ref, vc_ref, vs_ref, vw_ref, gt_ref, row_sc):
    tm, d = x_ref.shape
    sw = 2 * LANES
    s_idx = pl.program_id(1)

    def store_token_groups(ref, z):
        st = NSA_CMP_STRIDE
        for lt in range(sw // LANES):
            row_sc[lt] = z[:, lt * LANES:(lt + 1) * LANES]
        for j in range(st):
            for lt in range(sw // LANES):
                ref[:, j * sw + lt * LANES:j * sw + (lt + 1) * LANES] = (
                    row_sc[lt, pl.ds(j, tm // st, stride=st), :].astype(BF16))

    h = _rms(x_ref[...], g_ref[...]).astype(BF16)
    c2 = jnp.concatenate([c_ref[...]] * 2, axis=1)
    s1 = jnp.concatenate([s1_ref[...]] * 2, axis=1)
    s2 = jnp.concatenate([s2_ref[...]] * 2, axis=1)
    blk = (s_idx * tm + lax.broadcasted_iota(jnp.int32, (tm, 1), 0)) // NSA_SLC_BLOCK
    lane = lax.broadcasted_iota(jnp.int32, (tm, sw), 1) % LANES
    blk_id = jnp.where(lane - NSA_HEAD_DIM == blk, NEG_INF, 0.0)
    inv_dh = 1.0 / NSA_HEAD_DIM
    for sec in range(NSA_NORM_SECTIONS):
        z = _dot(h, wn_ref[:, sec * sw:(sec + 1) * sw])
        aug = sec >= 6
        ss = _dot((z * z).astype(BF16), bd_ref[1 if aug else 0]) * inv_dh
        z = z * lax.rsqrt(ss + RMS_EPS) * gain_ref[:, sec * sw:(sec + 1) * sw]
        z = z * c2 + pltpu.roll(z, sw - ROPE_DIMS // 2, axis=1) * s1 + pltpu.roll(z, ROPE_DIMS // 2, axis=1) * s2
        if sec < 4:
            q_ref[:, sec * sw:(sec + 1) * sw] = z.astype(BF16)
        elif sec == 4:
            store_token_groups(kc_ref, z)
        elif sec == 5:
            for g in range(NSA_KV_GROUPS):
                kw_ref[g] = z[:, g * NSA_HEAD_DIM:(g + 1) * NSA_HEAD_DIM].astype(BF16)
        else:
            ksa_ref[:, (sec - 6) * sw:(sec - 5) * sw] = (z + blk_id).astype(BF16)
    v = _dot(h, wv_ref[...])
    store_token_groups(vc_ref, v[:, 0:sw])
    ones_col = jnp.where(lax.broadcasted_iota(jnp.int32, (tm, NSA_HEAD_DIM), 1) == 0, 1.0, 0.0)
    for g in range(NSA_KV_GROUPS):
        vs_g = v[:, sw + g * NSA_HEAD_DIM:sw + (g + 1) * NSA_HEAD_DIM]
        vw_g = v[:, 2 * sw + g * NSA_HEAD_DIM:2 * sw + (g + 1) * NSA_HEAD_DIM]
        vs_ref[g] = jnp.concatenate([vs_g, ones_col], axis=1).astype(BF16)
        vw_ref[g] = jnp.concatenate([vw_g, ones_col], axis=1).astype(BF16)
    gt_ref[...] = jax.nn.sigmoid(_dot(h, wg_ref[...]))


def _nsa_proj(x3, g_mix, w_in, q_gain, k_gain, rope, tm=512):
    b, s, d = x3.shape
    hh, gg, dh = NSA_HEADS, NSA_KV_GROUPS, NSA_HEAD_DIM
    qd, gw = hh * dh, gg * dh
    tm = min(tm, s)
    wq = w_in[:, :qd]
    wkv = w_in[:, qd:qd + 6 * gw].reshape(d, 3, 2, gw)
    wgate = jnp.pad(w_in[:, qd + 6 * gw:], ((0, 0), (0, LANES - 3 * hh)))

    def spread(a):
        a = a.reshape(a.shape[:-1] + (gg, dh))
        return jnp.pad(a, [(0, 0)] * (a.ndim - 1) + [(0, LANES - dh)]).reshape(a.shape[:-2] + (gg * LANES,))

    w_norm = jnp.concatenate([wq, wkv[:, 0, 0], wkv[:, 2, 0], spread(wkv[:, 1, 0])], axis=1).astype(BF16)
    w_v = jnp.concatenate([wkv[:, 0, 1], wkv[:, 1, 1], wkv[:, 2, 1]], axis=1).astype(BF16)
    gain = jnp.concatenate([jnp.tile(q_gain, hh) * (dh ** -0.5 * LOG2_E), jnp.tile(k_gain[0], gg), jnp.tile(k_gain[2], gg),
                            spread(jnp.tile(k_gain[1], gg))]).reshape(1, -1).astype(F32)
    lane = np.arange(2 * LANES)
    bd = np.stack([(lane[:, None] // dh == lane[None, :] // dh), (lane[:, None] // LANES == lane[None, :] // LANES)])
    bd = jnp.asarray(bd.astype(np.float32), BF16)
    n_norm = NSA_NORM_SECTIONS * 2 * LANES
    rope_spec = pl.BlockSpec((tm, LANES), lambda i, j: (j, 0))
    tok = lambda w: pl.BlockSpec((None, tm, w), lambda i, j: (i, j, 0))
    grp = pl.BlockSpec((None, gg, tm, dh), lambda i, j: (i, 0, j, 0))
    grp_v = pl.BlockSpec((None, gg, tm, LANES), lambda i, j: (i, 0, j, 0))
    st = NSA_CMP_STRIDE
    cmp_in = pl.BlockSpec((None, tm // st, st * gw), lambda i, j: (i, j, 0))
    sds = jax.ShapeDtypeStruct
    return pl.pallas_call(
        _nsa_proj_body,
        grid=(b, s // tm),
        in_specs=[tok(d), _const_spec((1, d)), _const_spec((d, n_norm)), _const_spec((d, 3 * gw)),
                  _const_spec((d, LANES)), _const_spec((1, n_norm)), _const_spec((2, 2 * LANES, 2 * LANES)),
                  rope_spec, rope_spec, rope_spec],
        out_specs=[tok(qd), cmp_in, grp, tok(gg * LANES), cmp_in, grp_v, grp_v, tok(LANES)],
        out_shape=[sds((b, s, qd), BF16), sds((b, s // st, st * gw), BF16), sds((b, gg, s, dh), BF16),
                   sds((b, s, gg * LANES), BF16), sds((b, s // st, st * gw), BF16), sds((b, gg, s, LANES), BF16),
                   sds((b, gg, s, LANES), BF16), sds((b, s, LANES), F32)],
        scratch_shapes=[pltpu.VMEM((gw // LANES, tm, LANES), F32)],
        compiler_params=_cp("parallel", "parallel"),
        name="nsa_proj",
    )(x3, _row(g_mix), w_norm, w_v, wgate.astype(BF16), gain, bd, *rope)


def _nsa_compress_body(kin_ref, vin_ref, wa_ref, wb_ref, pa_ref, pb_ref, b1_ref, w2_ref, k_ref, vt_ref, q_sc, *, n_cmp):
    nr = kin_ref.shape[0]
    dh = NSA_HEAD_DIM
    row = lax.broadcasted_iota(jnp.int32, (nr, 1), 0)
    for c, in_ref in enumerate((kin_ref, vin_ref)):
        x = in_ref[...].astype(F32)
        first = _dot((x + pa_ref[c]).astype(BF16), wa_ref[c])
        q_sc[0:nr, :] = _dot((x + pb_ref[c]).astype(BF16), wb_ref[c])
        q_sc[nr:nr + SUBLANES, :] = jnp.zeros((SUBLANES, q_sc.shape[1]), F32)
        hid = jax.nn.gelu(first + q_sc[1:nr + 1, :] + b1_ref[c])
        comp = jnp.where(row < n_cmp, _dot(hid.astype(BF16), w2_ref[c]), 0.0)
        if c == 0:
            for g in range(NSA_KV_GROUPS):
                k_ref[g] = comp[:, g * dh:(g + 1) * dh].astype(BF16)
        else:
            comp_t = comp.T
            for g in range(NSA_KV_GROUPS):
                vt_ref[g] = comp_t[g * dh:(g + 1) * dh, :].astype(BF16)


def _nsa_compress(kc16, vc16, cmp_pos, cmp_w1, cmp_b1, cmp_w2):
    b, nr, _ = kc16.shape
    gg, dh, st = NSA_KV_GROUPS, NSA_HEAD_DIM, NSA_CMP_STRIDE
    gw = gg * dh
    s = nr * st
    n_cmp = (s - NSA_CMP_BLOCK) // st + 1
    eye = jnp.eye(gg, dtype=F32)
    w1 = cmp_w1.reshape(2, 2, st, dh, dh)
    wexp = jnp.einsum('chjde,gf->chjgdfe', w1, eye).reshape(2, 2, st * gw, gw).astype(BF16)
    pos = jnp.broadcast_to(cmp_pos.reshape(2, 2, st, 1, dh), (2, 2, st, gg, dh)).reshape(2, 2, 1, st * gw).astype(F32)
    b1 = jnp.tile(cmp_b1, (1, gg)).reshape(2, 1, gw).astype(F32)
    w2 = jnp.einsum('cde,gf->cgdfe', cmp_w2, eye).reshape(2, gw, gw).astype(BF16)
    full = lambda shape: _const_spec(shape)
    return pl.pallas_call(
        functools.partial(_nsa_compress_body, n_cmp=n_cmp),
        grid=(b,),
        in_specs=[pl.BlockSpec((None, nr, st * gw), lambda i: (i, 0, 0)),
                  pl.BlockSpec((None, nr, st * gw), lambda i: (i, 0, 0)),
                  full((2, st * gw, gw)), full((2, st * gw, gw)), full((2, 1, st * gw)), full((2, 1, st * gw)),
                  full((2, 1, gw)), full((2, gw, gw))],
        out_specs=[pl.BlockSpec((None, gg, nr, dh), lambda i: (i, 0, 0, 0)),
                   pl.BlockSpec((None, gg, dh, nr), lambda i: (i, 0, 0, 0))],
        out_shape=[jax.ShapeDtypeStruct((b, gg, nr, dh), BF16), jax.ShapeDtypeStruct((b, gg, dh, nr), BF16)],
        scratch_shapes=[pltpu.VMEM((nr + SUBLANES, gw), F32)],
        compiler_params=_cp("parallel"),
        name="nsa_compress",
    )(kc16, vc16, wexp[:, 0], wexp[:, 1], pos[:, 0], pos[:, 1], b1, w2)


def _stacked_row_index(n_rep, tq):
    assert tq & (tq - 1) == 0
    return jnp.bitwise_and(lax.broadcasted_iota(jnp.int32, (n_rep * tq, 1), 0), tq - 1)


def _nt_dot(a, b):
    return lax.dot_general(a, b, (((1,), (1,)), ((), ())), preferred_element_type=F32)


def _nsa_cmp_select_body(q_ref, kc_ref, vt_ref, ovl_ref, oc_ref, ns_ref, sc_sc, rank_sc):
    tq = q_ref.shape[0]
    nr = kc_ref.shape[0]
    dh = NSA_HEAD_DIM
    nb = NSA_MAX_SEL_BLOCKS
    q0 = pl.program_id(2) * tq
    kc, vt = kc_ref[...], vt_ref[...]
    t_lane = q0 + lax.broadcasted_iota(jnp.int32, (1, tq), 1)
    end_row = lax.broadcasted_iota(jnp.int32, (nr, 1), 0) * NSA_CMP_STRIDE + (NSA_CMP_BLOCK - 1)
    mask_t = end_row <= t_lane
    psum_t = jnp.zeros((nr, tq), F32)
    outs_t = []
    for h in range(NSA_REP):
        qh = q_ref[:, h * dh:(h + 1) * dh]
        s_t = jnp.where(mask_t, _nt_dot(kc, qh), NEG_INF)
        m_t = jnp.maximum(jnp.max(s_t, axis=0, keepdims=True), 0.1 * NEG_INF)
        e_t = jnp.exp2(s_t - m_t)
        den_t = jnp.sum(e_t, axis=0, keepdims=True)
        p_t = e_t * (1.0 / jnp.where(den_t > 0.0, den_t, 1.0))
        psum_t = psum_t + p_t
        outs_t.append(_dot(vt, p_t.astype(BF16)))
    oc_ref[...] = jnp.concatenate(outs_t, axis=0).T.astype(BF16)

    p_hi = psum_t.astype(BF16)
    p_lo = (psum_t - p_hi.astype(F32)).astype(BF16)
    imp = _dot(ovl_ref[...], p_hi) + _dot(ovl_ref[...], p_lo)
    blk = lax.broadcasted_iota(jnp.int32, (nb, 1), 0)
    cur = t_lane // NSA_SLC_BLOCK
    valid = blk <= cur
    forced = (blk == 0) | (blk >= cur - (NSA_LOCAL_BLOCKS - 1))
    sc_sc[...] = jnp.where(valid, jnp.where(forced, BIG, imp), -BIG)
    rank_sc[...] = jnp.zeros(rank_sc.shape, F32)
    n_tiles = nb // SUBLANES
    last_tile = ((q0 + tq - 1) // NSA_SLC_BLOCK) // SUBLANES
    sub = lax.broadcasted_iota(jnp.int32, (SUBLANES, 1), 0)

    def count_ahead(jt, a):
        rows = slice(a * SUBLANES, (a + 1) * SUBLANES)
        tile = sc_sc[rows, :]
        rank = rank_sc[rows, :]
        for j in range(jt * SUBLANES, (jt + 1) * SUBLANES):
            sj = sc_sc[j:j + 1, :]
            if a > jt:
                rank = rank + jnp.where(sj >= tile, 1.0, 0.0)
            elif a < jt:
                rank = rank + jnp.where(sj > tile, 1.0, 0.0)
            else:
                rank = rank + jnp.where(sub > j - a * SUBLANES,
                                        jnp.where(sj >= tile, 1.0, 0.0), jnp.where(sj > tile, 1.0, 0.0))
        rank_sc[rows, :] = rank

    for jt in range(n_tiles):
        @pl.when(jt <= last_tile)
        def _(jt=jt):
            for a in range(n_tiles):
                if a <= jt:
                    count_ahead(jt, a)
                else:
                    pl.when(a <= last_tile)(functools.partial(count_ahead, jt, a))

    notsel = jnp.where(rank_sc[...] < float(NSA_TOP_N), 0.0, 1.0)
    notsel = jnp.concatenate([notsel, jnp.ones((LANES - nb, tq), F32)], axis=0)
    ns_ref[...] = notsel.T[:, :nb].astype(BF16)


def _nsa_cmp_select(q, k_cmp, vt_cmp, tq=256):
    b, s, qd = q.shape
    gg, dh = NSA_KV_GROUPS, NSA_HEAD_DIM
    nr = k_cmp.shape[2]
    nb = NSA_MAX_SEL_BLOCKS
    tq = min(tq, s)
    assert s // NSA_SLC_BLOCK <= nb
    c_start = np.arange(nr) * NSA_CMP_STRIDE
    c_end = c_start + NSA_CMP_BLOCK - 1
    s_start = np.arange(nb) * NSA_SLC_BLOCK
    s_end = s_start + NSA_SLC_BLOCK - 1
    ovl = ((c_start[None, :] <= s_end[:, None]) & (c_end[None, :] >= s_start[:, None])).astype(np.float32)
    return pl.pallas_call(
        _nsa_cmp_select_body,
        grid=(b, gg, s // tq),
        in_specs=[pl.BlockSpec((None, tq, NSA_REP * dh), lambda i, g, j: (i, j, g)),
                  pl.BlockSpec((None, None, nr, dh), lambda i, g, j: (i, g, 0, 0)),
                  pl.BlockSpec((None, None, dh, nr), lambda i, g, j: (i, g, 0, 0)),
                  _const_spec((nb, nr))],
        out_specs=[pl.BlockSpec((None, tq, NSA_REP * dh), lambda i, g, j: (i, j, g)),
                   pl.BlockSpec((None, None, tq, nb), lambda i, g, j: (i, g, j, 0))],
        out_shape=[jax.ShapeDtypeStruct((b, s, qd), BF16), jax.ShapeDtypeStruct((b, gg, s, nb), BF16)],
        scratch_shapes=[pltpu.VMEM((nb, tq), F32), pltpu.VMEM((nb, tq), F32)],
        compiler_params=_cp("parallel", "parallel", "parallel"),
        name="nsa_cmp_select",
    )(q, k_cmp, vt_cmp, jnp.asarray(ovl, BF16))


def _nsa_sel_body(q_ref, ns_ref, k_ref, v_ref, o_ref, qa_sc, m_sc, acc_sc):
    tq = q_ref.shape[0]
    dh = NSA_HEAD_DIM
    qi = pl.program_id(2)
    for h in range(NSA_REP):
        qa_sc[h * tq:(h + 1) * tq, :] = jnp.concatenate([q_ref[:, h * dh:(h + 1) * dh], ns_ref[...]], axis=1)
    m_sc[...] = jnp.full(m_sc.shape, NEG_INF, F32)
    acc_sc[...] = jnp.zeros(acc_sc.shape, F32)
    t_pos = lax.broadcasted_iota(jnp.int32, (tq, 1), 0)
    k_pos = lax.broadcasted_iota(jnp.int32, (1, tq), 1)

    def key_tile(kt, diagonal):
        start = pl.multiple_of(kt * tq, tq)
        s = _nt_dot(qa_sc[...], k_ref[pl.ds(start, tq), :])
        v = v_ref[pl.ds(start, tq), :]
        for h in range(NSA_REP):
            sh = s[h * tq:(h + 1) * tq]
            if diagonal:
                sh = jnp.where(k_pos <= t_pos, sh, NEG_INF)
            m_old = m_sc[h]
            m_new = jnp.maximum(m_old, jnp.max(sh, axis=-1, keepdims=True))
            p = jnp.exp2(sh - jnp.concatenate([m_new] * (tq // LANES), axis=1))
            acc_sc[h] = jnp.exp2(m_old - m_new) * acc_sc[h] + _dot(p.astype(BF16), v)
            m_sc[h] = m_new

    def full_tile(kt, carry):
        key_tile(kt, False)
        return carry

    lax.fori_loop(0, qi, full_tile, 0)
    key_tile(qi, True)
    outs = []
    for h in range(NSA_REP):
        acc = acc_sc[h]
        outs.append(acc[:, :dh] / acc[:, dh:dh + 1])
    o_ref[...] = jnp.concatenate(outs, axis=1).astype(BF16)


def _nsa_selected(q, notsel, ksa, vs, tq=256):
    b, s, qd = q.shape
    gg, dh, rr = NSA_KV_GROUPS, NSA_HEAD_DIM, NSA_REP
    tq = min(tq, s)
    assert tq % LANES == 0
    return pl.pallas_call(
        _nsa_sel_body,
        grid=(b, gg, s // tq),
        in_specs=[pl.BlockSpec((None, tq, rr * dh), lambda i, g, j: (i, j, g)),
                  pl.BlockSpec((None, None, tq, NSA_MAX_SEL_BLOCKS), lambda i, g, j: (i, g, j, 0)),
                  pl.BlockSpec((None, s, LANES), lambda i, g, j: (i, 0, g)),
                  pl.BlockSpec((None, None, s, LANES), lambda i, g, j: (i, g, 0, 0))],
        out_specs=pl.BlockSpec((None, tq, rr * dh), lambda i, g, j: (i, j, g)),
        out_shape=jax.ShapeDtypeStruct((b, s, qd), BF16),
        scratch_shapes=[pltpu.VMEM((rr * tq, LANES), BF16), pltpu.VMEM((rr, tq, LANES), F32),
                        pltpu.VMEM((rr, tq, LANES), F32)],
        compiler_params=_cp("parallel", "parallel", "arbitrary"),
        name="nsa_selected",
    )(q, notsel, ksa, vs)


def _nsa_window_body(q_ref, *refs, n_kv):
    k_refs, v_refs, o_ref = refs[:n_kv], refs[n_kv:2 * n_kv], refs[2 * n_kv]
    tq = q_ref.shape[0]
    dh = NSA_HEAD_DIM
    qi = pl.program_id(2)
    qs = jnp.concatenate([q_ref[:, h * dh:(h + 1) * dh] for h in range(NSA_REP)], axis=0)
    t_pos = lax.broadcasted_iota(jnp.int32, (tq, 1), 0)
    k_rel = lax.broadcasted_iota(jnp.int32, (1, tq), 1)
    scores = [_nt_dot(qs, k_refs[j][...]) for j in range(n_kv)]
    outs = []
    for h in range(NSA_REP):
        parts = []
        for j in range(n_kv):
            sh = scores[j][h * tq:(h + 1) * tq]
            back = (n_kv - 1 - j) * tq
            in_seq = qi >= n_kv - 1 - j
            if j == n_kv - 1:
                parts.append(jnp.where(k_rel <= t_pos, sh, NEG_INF))
            elif j == 0:
                limit = jnp.where(in_seq, NSA_WINDOW, -(2 ** 30))
                parts.append(jnp.where(t_pos - k_rel + back < limit, sh, NEG_INF))
            else:
                parts.append(sh + jnp.where(in_seq, 0.0, NEG_INF))
        m = functools.reduce(jnp.maximum, [jnp.max(p_, axis=-1, keepdims=True) for p_ in parts])
        parts = [jnp.exp2(p_ - m) for p_ in parts]
        o = functools.reduce(jnp.add, [_dot(parts[j].astype(BF16), v_refs[j][...]) for j in range(n_kv)])
        outs.append(o[:, :dh] / o[:, dh:dh + 1])
    o_ref[...] = jnp.concatenate(outs, axis=1).astype(BF16)


def _nsa_window(q, kw, vw, tq=256):
    b, s, qd = q.shape
    gg, dh, rr = NSA_KV_GROUPS, NSA_HEAD_DIM, NSA_REP
    tq = min(tq, s)
    assert NSA_WINDOW % tq == 0
    n_kv = NSA_WINDOW // tq + 1
    kv_spec = lambda off, w: pl.BlockSpec((None, None, tq, w), lambda i, g, j: (i, g, jnp.maximum(j - off, 0), 0))
    k_specs = [kv_spec(n_kv - 1 - j, dh) for j in range(n_kv)]
    v_specs = [kv_spec(n_kv - 1 - j, LANES) for j in range(n_kv)]
    return pl.pallas_call(
        functools.partial(_nsa_window_body, n_kv=n_kv),
        grid=(b, gg, s // tq),
        in_specs=[pl.BlockSpec((None, tq, rr * dh), lambda i, g, j: (i, j, g))] + k_specs + v_specs,
        out_specs=pl.BlockSpec((None, tq, rr * dh), lambda i, g, j: (i, j, g)),
        out_shape=jax.ShapeDtypeStruct((b, s, qd), BF16),
        compiler_params=_cp("parallel", "parallel", "parallel"),
        name="nsa_window",
    )(q, *([kw] * n_kv), *([vw] * n_kv))


def _nsa_out_body(x_ref, oc_ref, os_ref, ow_ref, gt_ref, ex_ref, w_ref, o_ref):
    qd = oc_ref.shape[1]
    gt = gt_ref[...]
    g_hi = gt.astype(BF16)
    g_lo = (gt - g_hi.astype(F32)).astype(BF16)
    o = jnp.zeros(oc_ref.shape, F32)
    for br, ref in enumerate((oc_ref, os_ref, ow_ref)):
        ex = ex_ref[:, br * qd:(br + 1) * qd]
        o = o + (_dot(g_hi, ex) + _dot(g_lo, ex)) * ref[...].astype(F32)
    o_ref[...] = x_ref[...] + _dot(o.astype(BF16), w_ref[...])


def _nsa_out(x2, oc, osel, ow, gates, w_out, tm=512):
    t, d = x2.shape
    qd = oc.shape[1]
    tm = min(tm, t)
    col = np.arange(3 * qd)
    expand = (np.arange(LANES)[:, None] == (col // qd) * NSA_HEADS + (col % qd) // NSA_HEAD_DIM).astype(np.float32)
    tok = lambda w: pl.BlockSpec((tm, w), lambda i: (i, 0))
    return pl.pallas_call(
        _nsa_out_body,
        grid=(t // tm,),
        in_specs=[tok(d), tok(qd), tok(qd), tok(qd), tok(LANES), _const_spec((LANES, 3 * qd)), _const_spec((qd, d))],
        out_specs=tok(d),
        out_shape=jax.ShapeDtypeStruct((t, d), F32),
        compiler_params=_cp("parallel"),
        name="nsa_out",
    )(x2, oc, osel, ow, gates, jnp.asarray(expand, BF16), w_out.astype(BF16))


def _nsa_layer(x3, positions, g_mix, w_in, q_gain, k_gain, cmp_pos, cmp_w1, cmp_b1, cmp_w2, w_out):
    b, s, d = x3.shape
    rope = _rope_tables(positions)
    q, kc, kw, ksa, vc, vs, vw, gates = _nsa_proj(x3, g_mix, w_in, q_gain, k_gain, rope)
    k_cmp, vt_cmp = _nsa_compress(kc, vc, cmp_pos, cmp_w1, cmp_b1, cmp_w2)
    oc, notsel = _nsa_cmp_select(q, k_cmp, vt_cmp)
    osel = _nsa_selected(q, notsel, ksa, vs)
    ow = _nsa_window(q, kw, vw)
    t = b * s
    flat = lambda a: a.reshape(t, a.shape[-1])
    return _nsa_out(x3.reshape(t, d), flat(oc), flat(osel), flat(ow), flat(gates), w_out).reshape(b, s, d)


def kernel(x, positions, norm_mix, norm_ffn, ffn_w_in, ffn_w_out, conv_w_in, conv_b_in, conv_w_dw, conv_b_dw, conv_ln_g, conv_ln_b, conv_w_out, nsa_w_in, nsa_q_gain, nsa_k_gain, nsa_cmp_pos, nsa_cmp_w1, nsa_cmp_b1, nsa_cmp_w2, nsa_w_out, s5_lam_re, s5_lam_im, s5_log_step, s5_b_re, s5_b_im, s5_c_re, s5_c_im, s5_d, s5_w_glu, s5_b_glu, pool_w, pool_scale):
    b, s, d = x.shape
    depth = norm_mix.shape[0]
    for i in range(depth):
        m, j = i % N_MIXERS, i // N_MIXERS
        if m == 0:
            x = _conv_layer(x, norm_mix[i], conv_w_in[j], conv_b_in[j], conv_w_dw[j], conv_b_dw[j],
                            conv_ln_g[j], conv_ln_b[j], conv_w_out[j])
        elif m == 1:
            x = _nsa_layer(x, positions, norm_mix[i], nsa_w_in[j], nsa_q_gain[j], nsa_k_gain[j], nsa_cmp_pos[j],
                           nsa_cmp_w1[j], nsa_cmp_b1[j], nsa_cmp_w2[j], nsa_w_out[j])
        elif m == 2:
            x = _s5_layer(x, norm_mix[i], s5_lam_re[j], s5_lam_im[j], s5_log_step[j], s5_b_re[j], s5_b_im[j],
                          s5_c_re[j], s5_c_im[j], s5_d[j], s5_w_glu[j], s5_b_glu[j])
        elif m == 3:
            x = _pool_layer(x, norm_mix[i], pool_w[j], pool_scale[j])
        x = _ffn(x.reshape(b * s, d), norm_ffn[i], ffn_w_in[i], ffn_w_out[i]).reshape(b, s, d)
    return x
```

```python
import functools
import math

import numpy as np
import jax
import jax.numpy as jnp
from jax import lax
from jax.experimental import pallas as pl
from jax.experimental.pallas import tpu as pltpu

F32 = jnp.float32
BF16 = jnp.bfloat16

RMS_EPS = 1e-6
LN_EPS = 1e-5
N_MIXERS = 4
CONV_WIDTH = 31
POOL_WINDOWS = (2, 4, 8, 16)

V7X_VMEM_LIMIT_BYTES = 56 * 1024 * 1024
SUBLANES = 8
LANES = 128


def _cp(*sem):
    return pltpu.CompilerParams(dimension_semantics=sem, vmem_limit_bytes=V7X_VMEM_LIMIT_BYTES)


def _const_spec(shape):
    nd = len(shape)
    return pl.BlockSpec(shape, lambda *_: (0,) * nd, pipeline_mode=pl.Buffered(1))


def _dot(a, b):
    return jnp.dot(a, b, preferred_element_type=F32)


def _rms(x, g):
    ms = jnp.mean(x * x, axis=-1, keepdims=True)
    return x * lax.rsqrt(ms + RMS_EPS) * g


def _row(v):
    return v.reshape(1, -1).astype(F32)


FFN_CHUNK = 256


def _ffn_body(x_ref, g_ref, wg_ref, wu_ref, wo_ref, o_ref, act_sc):
    x = x_ref[...]
    h = _rms(x, g_ref[...]).astype(BF16)
    n_chunks = act_sc.shape[1] // FFN_CHUNK
    for c in range(n_chunks):
        sl = slice(c * FFN_CHUNK, (c + 1) * FFN_CHUNK)
        gate = _dot(h, wg_ref[:, sl])
        up = _dot(h, wu_ref[:, sl])
        act_sc[:, sl] = (gate * jax.nn.sigmoid(gate) * up).astype(BF16)
    o_ref[...] = x + _dot(act_sc[...], wo_ref[...])


def _ffn(x2, g, w_in, w_out, tm=512):
    t, d = x2.shape
    f = w_out.shape[0]
    w_in = w_in.astype(BF16)
    w_out = w_out.astype(BF16)
    return pl.pallas_call(
        _ffn_body,
        grid=(t // tm,),
        in_specs=[
            pl.BlockSpec((tm, d), lambda i: (i, 0)),
            _const_spec((1, d)),
            pl.BlockSpec((d, f), lambda i: (0, 0), pipeline_mode=pl.Buffered(1)),
            pl.BlockSpec((d, f), lambda i: (0, 1), pipeline_mode=pl.Buffered(1)),
            _const_spec((f, d)),
        ],
        out_specs=pl.BlockSpec((tm, d), lambda i: (i, 0)),
        out_shape=jax.ShapeDtypeStruct((t, d), F32),
        scratch_shapes=[pltpu.VMEM((tm, f), BF16)],
        compiler_params=_cp("parallel"),
        name="ffn",
    )(x2, _row(g), w_in, w_in, w_out)


CONV_HALO = 32
CONV_ROWS = 64


def _conv_body(x_ref, g_ref, wa_ref, wg_ref, ba_ref, bg_ref, wdw_ref, bdw_ref,
               lng_ref, lnb_ref, wo_ref, o_ref, u_sc, v_sc):
    tm, d = x_ref.shape
    s = pl.program_id(1)

    @pl.when(s == 0)
    def _():
        u_sc[0:CONV_HALO, :] = jnp.zeros((CONV_HALO, d), F32)

    @pl.when(s > 0)
    def _():
        u_sc[0:CONV_HALO, :] = u_sc[tm:tm + CONV_HALO, :]

    x = x_ref[...]
    h = _rms(x, g_ref[...]).astype(BF16)
    a = _dot(h, wa_ref[...]) + ba_ref[...]
    gt = _dot(h, wg_ref[...]) + bg_ref[...]
    u_sc[CONV_HALO:CONV_HALO + tm, :] = a * jax.nn.sigmoid(gt)

    off = CONV_HALO - (CONV_WIDTH - 1)

    def conv_rows(c, carry):
        r0 = pl.multiple_of(c * CONV_ROWS, CONV_ROWS)
        for lt in range(d // LANES):
            lanes = slice(lt * LANES, (lt + 1) * LANES)
            win = u_sc[pl.ds(r0, CONV_ROWS + CONV_HALO), lanes]
            acc = jnp.zeros((CONV_ROWS, LANES), F32) + bdw_ref[:, lanes]
            for sft in range(SUBLANES):
                shifted = pltpu.roll(win, CONV_ROWS + CONV_HALO - sft, axis=0) if sft else win
                for j in range(sft, CONV_HALO + 1, SUBLANES):
                    k = j - off
                    if 0 <= k < CONV_WIDTH:
                        acc = acc + wdw_ref[k:k + 1, lanes] * shifted[j - sft:j - sft + CONV_ROWS]
            v_sc[pl.ds(r0, CONV_ROWS), lanes] = acc
        return carry

    lax.fori_loop(0, tm // CONV_ROWS, conv_rows, 0)

    v = v_sc[...]
    mu = jnp.mean(v, axis=-1, keepdims=True)
    vc = v - mu
    var = jnp.mean(vc * vc, axis=-1, keepdims=True)
    y = vc * lax.rsqrt(var + LN_EPS) * lng_ref[...] + lnb_ref[...]
    y = (y * jax.nn.sigmoid(y)).astype(BF16)
    o_ref[...] = x + _dot(y, wo_ref[...])


def _conv_layer(x3, g, w_in, b_in, w_dw, b_dw, ln_g, ln_b, w_out, tm=256):
    b, s, d = x3.shape
    w_in = w_in.astype(BF16)
    return pl.pallas_call(
        _conv_body,
        grid=(b, s // tm),
        in_specs=[
            pl.BlockSpec((None, tm, d), lambda i, j: (i, j, 0)),
            _const_spec((1, d)),
            pl.BlockSpec((d, d), lambda i, j: (0, 0), pipeline_mode=pl.Buffered(1)),
            pl.BlockSpec((d, d), lambda i, j: (0, 1), pipeline_mode=pl.Buffered(1)),
            pl.BlockSpec((1, d), lambda i, j: (0, 0), pipeline_mode=pl.Buffered(1)),
            pl.BlockSpec((1, d), lambda i, j: (0, 1), pipeline_mode=pl.Buffered(1)),
            _const_spec((CONV_WIDTH, d)),
            _const_spec((1, d)),
            _const_spec((1, d)),
            _const_spec((1, d)),
            _const_spec((d, d)),
        ],
        out_specs=pl.BlockSpec((None, tm, d), lambda i, j: (i, j, 0)),
        out_shape=jax.ShapeDtypeStruct((b, s, d), F32),
        scratch_shapes=[pltpu.VMEM((CONV_HALO + tm, d), F32), pltpu.VMEM((tm, d), F32)],
        compiler_params=_cp("parallel", "arbitrary"),
        name="conv_module",
    )(x3, _row(g), w_in, w_in, _row(b_in), _row(b_in), w_dw.astype(F32), _row(b_dw),
      _row(ln_g), _row(ln_b), w_out.astype(BF16))


POOL_HALO = 16


def _pool_body(x_ref, g_ref, w_ref, sc_ref, o_ref, h_sc):
    tm, d = x_ref.shape
    s = pl.program_id(1)
    grp = d // len(POOL_WINDOWS)

    @pl.when(s == 0)
    def _():
        h_sc[0:POOL_HALO, :] = jnp.zeros((POOL_HALO, d), F32)

    @pl.when(s > 0)
    def _():
        h_sc[0:POOL_HALO, :] = h_sc[tm:tm + POOL_HALO, :]

    x = x_ref[...]
    h_sc[POOL_HALO:POOL_HALO + tm, :] = _rms(x, g_ref[...])
    t = s * tm + lax.broadcasted_iota(jnp.int32, (tm, 1), 0)
    for gi, win in enumerate(POOL_WINDOWS):
        lo = gi * grp
        hcur = h_sc[POOL_HALO:POOL_HALO + tm, lo:lo + grp]
        tot = hcur
        for k in range(1, win):
            tot = tot + h_sc[POOL_HALO - k:POOL_HALO - k + tm, lo:lo + grp]
        cnt = jnp.minimum(t + 1, win).astype(F32)
        pooled = (tot * (1.0 / cnt) - hcur).astype(BF16)
        y = _dot(pooled, w_ref[gi]) * sc_ref[:, lo:lo + grp]
        o_ref[:, lo:lo + grp] = x[:, lo:lo + grp] + y


def _pool_layer(x3, g, w_grp, scale, tm=512):
    b, s, d = x3.shape
    ng, grp, _ = w_grp.shape
    return pl.pallas_call(
        _pool_body,
        grid=(b, s // tm),
        in_specs=[
            pl.BlockSpec((None, tm, d), lambda i, j: (i, j, 0)),
            _const_spec((1, d)),
            _const_spec((ng, grp, grp)),
            _const_spec((1, d)),
        ],
        out_specs=pl.BlockSpec((None, tm, d), lambda i, j: (i, j, 0)),
        out_shape=jax.ShapeDtypeStruct((b, s, d), F32),
        scratch_shapes=[pltpu.VMEM((POOL_HALO + tm, d), F32)],
        compiler_params=_cp("parallel", "arbitrary"),
        name="pool",
    )(x3, _row(g), w_grp.astype(BF16), _row(scale))


S5_L = 8
S5_GPT = 8


def _s5_operators(lam_re, lam_im, log_step, b_re, b_im, c_re, c_im):
    hi = lax.Precision.HIGHEST
    n_g, n_p = lam_re.shape
    n_c = b_re.shape[-1]
    n_j = n_g // S5_GPT
    L = S5_L
    def cmul(a, b):
        return a[0] * b[0] - a[1] * b[1], a[0] * b[1] + a[1] * b[0]

    def cexp(z):
        mag = jnp.exp(z[0])
        return mag * jnp.cos(z[1]), mag * jnp.sin(z[1])

    lr, li = lam_re.astype(F32), lam_im.astype(F32)
    step = jnp.exp(log_step.astype(F32))[:, None]
    lam_bar = cexp((lr * step, li * step))
    inv_den = 1.0 / (lr * lr + li * li)
    zoh = cmul((lam_bar[0] - 1.0, lam_bar[1]), (lr * inv_den, -li * inv_den))
    b_bar = cmul((zoh[0][:, :, None], zoh[1][:, :, None]), (b_re.astype(F32), b_im.astype(F32)))
    c_mat = (c_re.astype(F32), c_im.astype(F32))
    dd = jnp.arange(2 * L, dtype=F32)[:, None, None]
    pw = cexp(((lr * step)[None] * dd, (li * step)[None] * dd))

    pb = cmul((pw[0][:L, :, :, None], pw[1][:L, :, :, None]), (b_bar[0][None], b_bar[1][None]))
    kern = (jnp.einsum('gap,dgpc->gdac', c_mat[0], pb[0], precision=hi)
            - jnp.einsum('gap,dgpc->gdac', c_mat[1], pb[1], precision=hi))
    sig = np.arange(L)[:, None]
    tau = np.arange(L)[None, :]
    lag = np.clip(tau - sig, 0, L - 1)
    kt = kern[:, lag] * jnp.asarray((tau >= sig).astype(np.float32))[None, :, :, None, None]
    kt = kt.reshape(n_j, S5_GPT, L, L, n_c, n_c)
    m_op = kt.transpose(0, 2, 1, 5, 3, 4).reshape(n_j, L * S5_GPT * n_c, L * n_c)

    e = jnp.stack(pb, axis=0)[:, ::-1].transpose(0, 2, 1, 3, 4)
    e = e.reshape(2, n_j, S5_GPT, L, n_p, n_c)
    b_op = e.transpose(1, 3, 2, 5, 0, 4).reshape(n_j, L * S5_GPT * n_c, 2 * n_p)

    f = cmul((c_mat[0][:, None], c_mat[1][:, None]),
             (pw[0][1:L + 1].transpose(1, 0, 2)[:, :, None, :], pw[1][1:L + 1].transpose(1, 0, 2)[:, :, None, :]))
    f = jnp.stack([f[0], -f[1]], axis=0).reshape(2, n_j, S5_GPT, L, n_c, n_p)
    c_op = f.transpose(1, 0, 2, 5, 3, 4).reshape(n_j, 2 * S5_GPT * n_p, L * n_c)

    ak = cexp(((lr * step * L)[None] * dd, (li * step * L)[None] * dd))
    ak = jnp.stack(ak, axis=1).reshape(2 * L, 2, n_j, S5_GPT * n_p)
    a_pw = ak.transpose(2, 0, 1, 3).reshape(n_j, 2 * L, 2 * S5_GPT * n_p)
    return m_op.astype(BF16), b_op.astype(BF16), c_op.astype(BF16), a_pw.astype(F32)


def _s5_regroup_body(x_ref, g_ref, o_ref, u_sc):
    tm, d = x_ref.shape
    u = _rms(x_ref[...], g_ref[...])
    rows = tm // S5_L
    for j in range(d // LANES):
        u_sc[j] = u[:, j * LANES:(j + 1) * LANES]
    for j in range(d // LANES):
        for sg in range(S5_L):
            o_ref[j, :, sg * LANES:(sg + 1) * LANES] = (
                u_sc[j, pl.ds(sg, rows, stride=S5_L), :].astype(BF16))


def _s5_regroup(x2, g, tm=512):
    t, d = x2.shape
    n_j = d // LANES
    return pl.pallas_call(
        _s5_regroup_body,
        grid=(t // tm,),
        in_specs=[pl.BlockSpec((tm, d), lambda i: (i, 0)), _const_spec((1, d))],
        out_specs=pl.BlockSpec((n_j, tm // S5_L, S5_L * LANES), lambda i: (0, i, 0)),
        out_shape=jax.ShapeDtypeStruct((n_j, t // S5_L, S5_L * LANES), BF16),
        scratch_shapes=[pltpu.VMEM((n_j, tm, LANES), F32)],
        compiler_params=_cp("parallel"),
        name="s5_regroup",
    )(x2, _row(g))


def _s5_chunk_body(u_ref, mc_ref, bc_ref, cc_ref, a_ref, rep_ref, y_ref,
                   m_ref, b_ref, c_ref, xr_sc, xi_sc, pr_sc, pi_sc, cr_sc, ci_sc):
    rows, width = u_ref.shape
    half = width // 2
    nt = pl.program_id(2)

    @pl.when((pl.program_id(1) == 0) & (nt == 0))
    def _():
        n_c = width // (S5_L * S5_GPT)
        n_p = half // S5_GPT
        cw = 2 * LANES
        for op_ref, src_ref, rep, row_div, col_div in (
                (m_ref, mc_ref, 0, n_c, n_c), (b_ref, bc_ref, 1, n_c, n_p), (c_ref, cc_ref, 0, n_p, n_c)):
            g_row = (lax.broadcasted_iota(jnp.int32, (width, 1), 0) // row_div) % S5_GPT
            for c0 in range(0, width, cw):
                g_col = ((c0 + lax.broadcasted_iota(jnp.int32, (1, cw), 1)) // col_div) % S5_GPT
                full = _dot(src_ref[...], rep_ref[rep, :, c0:c0 + cw])
                op_ref[:, c0:c0 + cw] = jnp.where(g_row == g_col, full, 0.0).astype(BF16)

    @pl.when(nt == 0)
    def _():
        cr_sc[...] = jnp.zeros_like(cr_sc)
        ci_sc[...] = jnp.zeros_like(ci_sc)

    u = u_ref[...]
    xin = _dot(u, b_ref[...])
    xr, xi = xin[:, :half], xin[:, half:]
    row = lax.broadcasted_iota(jnp.int32, (rows, 1), 0) % SUBLANES
    for shift in (1, 2, 4):
        ar, ai = a_ref[shift:shift + 1, :half], a_ref[shift:shift + 1, half:]
        keep = row >= shift
        sr = jnp.where(keep, pltpu.roll(xr, shift, axis=0), 0.0)
        si = jnp.where(keep, pltpu.roll(xi, shift, axis=0), 0.0)
        xr, xi = xr + ar * sr - ai * si, xi + ar * si + ai * sr
    xr_sc[...] = xr
    xi_sc[...] = xi
    keep = row >= 1
    pr_sc[...] = jnp.where(keep, pltpu.roll(xr, 1, axis=0), 0.0)
    pi_sc[...] = jnp.where(keep, pltpu.roll(xi, 1, axis=0), 0.0)

    akr, aki = a_ref[0:SUBLANES, :half], a_ref[0:SUBLANES, half:]
    a8r, a8i = a_ref[SUBLANES:SUBLANES + 1, :half], a_ref[SUBLANES:SUBLANES + 1, half:]

    def tile_step(t, carry):
        cr, ci = carry
        base = pl.multiple_of(t * SUBLANES, SUBLANES)
        pr_sc[pl.ds(base, SUBLANES), :] = pr_sc[pl.ds(base, SUBLANES), :] + akr * cr - aki * ci
        pi_sc[pl.ds(base, SUBLANES), :] = pi_sc[pl.ds(base, SUBLANES), :] + akr * ci + aki * cr
        lr = xr_sc[pl.ds(base + SUBLANES - 1, 1), :]
        li = xi_sc[pl.ds(base + SUBLANES - 1, 1), :]
        return lr + a8r * cr - a8i * ci, li + a8r * ci + a8i * cr

    cr, ci = lax.fori_loop(0, rows // SUBLANES, tile_step, (cr_sc[...], ci_sc[...]))
    cr_sc[...] = cr
    ci_sc[...] = ci

    y = _dot(u, m_ref[...])
    y = y + _dot(pr_sc[...].astype(BF16), c_ref[0:half, :])
    y = y + _dot(pi_sc[...].astype(BF16), c_ref[half:, :])
    y_ref[...] = y


def _s5_chunks(u3, m_op, b_op, c_op, a_pw, bsz):
    n_j, n_rows, width = u3.shape
    rows_per_seq = n_rows // bsz
    rows = min(512, rows_per_seq)
    nt = rows_per_seq // rows
    half = width // 2
    narrow = m_op.shape[-1]
    col = np.arange(width)
    n_c, n_p = width // (S5_L * S5_GPT), half // S5_GPT
    src_ta = (col // (S5_GPT * n_c)) * n_c + col % n_c
    src_rp = (col // (S5_GPT * n_p)) * n_p + col % n_p
    rep = np.stack([np.arange(narrow)[:, None] == src_ta[None, :], np.arange(narrow)[:, None] == src_rp[None, :]])
    wspec = lambda shape: pl.BlockSpec((None,) + shape, lambda j, b, t: (j, 0, 0), pipeline_mode=pl.Buffered(1))
    return pl.pallas_call(
        _s5_chunk_body,
        grid=(n_j, bsz, nt),
        in_specs=[
            pl.BlockSpec((None, rows, width), lambda j, b, t: (j, b * nt + t, 0)),
            wspec((width, narrow)), wspec((width, narrow)), wspec((width, narrow)),
            wspec((2 * S5_L, width)), _const_spec((2, narrow, width)),
        ],
        out_specs=pl.BlockSpec((None, rows, width), lambda j, b, t: (j, b * nt + t, 0)),
        out_shape=jax.ShapeDtypeStruct((n_j, n_rows, width), F32),
        scratch_shapes=([pltpu.VMEM((width, width), BF16)] * 3 + [pltpu.VMEM((rows, half), F32)] * 4
                        + [pltpu.VMEM((1, half), F32)] * 2),
        compiler_params=_cp("arbitrary", "arbitrary", "arbitrary"),
        name="s5_chunks",
    )(u3, m_op, b_op, c_op, a_pw, jnp.asarray(rep.astype(np.float32), BF16))


def _s5_out_body(x_ref, y3_ref, g_ref, d_ref, w_ref, b_ref, o_ref, y_sc):
    tm, d = x_ref.shape
    rows = tm // S5_L
    for j in range(d // LANES):
        for tau in range(S5_L):
            y_sc[j, pl.ds(tau, rows, stride=S5_L), :] = y3_ref[j, :, tau * LANES:(tau + 1) * LANES]
    x = x_ref[...]
    u = _rms(x, g_ref[...])
    y_ssm = jnp.concatenate([y_sc[j] for j in range(d // LANES)], axis=1)
    y = jax.nn.gelu(y_ssm + d_ref[...] * u).astype(BF16)
    z = _dot(y, w_ref[...]) + b_ref[...]
    o_ref[...] = x + z[:, :d] * jax.nn.sigmoid(z[:, d:])


def _s5_out(x2, y3, g, d_skip, w_glu, b_glu, tm=512):
    t, d = x2.shape
    n_j = d // LANES
    return pl.pallas_call(
        _s5_out_body,
        grid=(t // tm,),
        in_specs=[
            pl.BlockSpec((tm, d), lambda i: (i, 0)),
            pl.BlockSpec((n_j, tm // S5_L, S5_L * LANES), lambda i: (0, i, 0)),
            _const_spec((1, d)), _const_spec((1, d)),
            _const_spec((d, 2 * d)), _const_spec((1, 2 * d)),
        ],
        out_specs=pl.BlockSpec((tm, d), lambda i: (i, 0)),
        out_shape=jax.ShapeDtypeStruct((t, d), F32),
        scratch_shapes=[pltpu.VMEM((n_j, tm, LANES), F32)],
        compiler_params=_cp("parallel"),
        name="s5_out",
    )(x2, y3, _row(g), _row(d_skip), w_glu.astype(BF16), _row(b_glu))


def _s5_layer(x3, g, lam_re, lam_im, log_step, b_re, b_im, c_re, c_im, d_skip, w_glu, b_glu):
    b, s, d = x3.shape
    x2 = x3.reshape(b * s, d)
    m_op, b_op, c_op, a_pw = _s5_operators(lam_re, lam_im, log_step, b_re, b_im, c_re, c_im)
    u3 = _s5_regroup(x2, g)
    y3 = _s5_chunks(u3, m_op, b_op, c_op, a_pw, b)
    return _s5_out(x2, y3, g, d_skip, w_glu, b_glu).reshape(b, s, d)


NSA_HEAD_DIM = 64
NSA_HEADS = 16
NSA_KV_GROUPS = 4
NSA_REP = NSA_HEADS // NSA_KV_GROUPS
NSA_CMP_BLOCK = 32
NSA_CMP_STRIDE = 16
NSA_SLC_BLOCK = 64
NSA_TOP_N = 16
NSA_LOCAL_BLOCKS = 2
NSA_WINDOW = 512
ROPE_THETA = 500000.0
ROPE_DIMS = NSA_HEAD_DIM // 4
NEG_INF = -1e30
BIG = 1e9
LOG2_E = 1.4426950408889634
NSA_MAX_SEL_BLOCKS = 64
NSA_NORM_SECTIONS = 8


def _rope_body(pos_ref, inv_ref, c_ref, s1_ref, s2_ref):
    ang = pos_ref[...].astype(F32) * inv_ref[...]
    lane = lax.broadcasted_iota(jnp.int32, ang.shape, 1) % NSA_HEAD_DIM
    cos, sin = jnp.cos(ang), jnp.sin(ang)
    half = ROPE_DIMS // 2
    c_ref[...] = jnp.where(lane < ROPE_DIMS, cos, 1.0)
    s1_ref[...] = jnp.where(lane < half, -sin, 0.0)
    s2_ref[...] = jnp.where(lane < half, 0.0, jnp.where(lane < ROPE_DIMS, sin, 0.0))


def _rope_tables(positions):
    s = positions.shape[0]
    half = ROPE_DIMS // 2
    inv_freq = ROPE_THETA ** (-jnp.arange(half, dtype=F32) / half)
    lane = np.arange(LANES) % NSA_HEAD_DIM
    inv_lane = jnp.where(jnp.asarray(lane < ROPE_DIMS), inv_freq[lane % half], 0.0).reshape(1, LANES)
    ts = min(s, 512)
    spec = pl.BlockSpec((ts, LANES), lambda i: (i, 0))
    return pl.pallas_call(
        _rope_body,
        grid=(s // ts,),
        in_specs=[pl.BlockSpec((ts, 1), lambda i: (i, 0)), _const_spec((1, LANES))],
        out_specs=[spec, spec, spec],
        out_shape=[jax.ShapeDtypeStruct((s, LANES), F32)] * 3,
        compiler_params=_cp("parallel"),
        name="rope_tables",
    )(positions.reshape(s, 1), inv_lane)


def _nsa_proj_body(x_ref, g_ref, wn_ref, wv_ref, wg_ref, gain_ref, bd_ref, c_ref, s1_ref, s2_ref,
                   q_ref, kc_ref, kw_ref, ksa_ref, vc_ref, vs_ref, vw_ref, gt_ref, row_sc):
    tm, d = x_ref.shape
    sw = 2 * LANES
    s_idx = pl.program_id(1)

    def store_token_groups(ref, z):
        st = NSA_CMP_STRIDE
        for lt in range(sw // LANES):
            row_sc[lt] = z[:, lt * LANES:(lt + 1) * LANES]
        for j in range(st):
            for lt in range(sw // LANES):
                ref[:, j * sw + lt * LANES:j * sw + (lt + 1) * LANES] = (
                    row_sc[lt, pl.ds(j, tm // st, stride=st), :].astype(BF16))

    h = _rms(x_ref[...], g_ref[...]).astype(BF16)
    c2 = jnp.concatenate([c_ref[...]] * 2, axis=1)
    s1 = jnp.concatenate([s1_ref[...]] * 2, axis=1)
    s2 = jnp.concatenate([s2_ref[...]] * 2, axis=1)
    blk = (s_idx * tm + lax.broadcasted_iota(jnp.int32, (tm, 1), 0)) // NSA_SLC_BLOCK
    lane = lax.broadcasted_iota(jnp.int32, (tm, sw), 1) % LANES
    blk_id = jnp.where(lane - NSA_HEAD_DIM == blk, NEG_INF, 0.0)
    inv_dh = 1.0 / NSA_HEAD_DIM
    for sec in range(NSA_NORM_SECTIONS):
        z = _dot(h, wn_ref[:, sec * sw:(sec + 1) * sw])
        aug = sec >= 6
        ss = _dot((z * z).astype(BF16), bd_ref[1 if aug else 0]) * inv_dh
        z = z * lax.rsqrt(ss + RMS_EPS) * gain_ref[:, sec * sw:(sec + 1) * sw]
        z = z * c2 + pltpu.roll(z, sw - ROPE_DIMS // 2, axis=1) * s1 + pltpu.roll(z, ROPE_DIMS // 2, axis=1) * s2
        if sec < 4:
            q_ref[:, sec * sw:(sec + 1) * sw] = z.astype(BF16)
        elif sec == 4:
            store_token_groups(kc_ref, z)
        elif sec == 5:
            for g in range(NSA_KV_GROUPS):
                kw_ref[g] = z[:, g * NSA_HEAD_DIM:(g + 1) * NSA_HEAD_DIM].astype(BF16)
        else:
            ksa_ref[:, (sec - 6) * sw:(sec - 5) * sw] = (z + blk_id).astype(BF16)
    v = _dot(h, wv_ref[...])
    store_token_groups(vc_ref, v[:, 0:sw])
    ones_col = jnp.where(lax.broadcasted_iota(jnp.int32, (tm, NSA_HEAD_DIM), 1) == 0, 1.0, 0.0)
    for g in range(NSA_KV_GROUPS):
        vs_g = v[:, sw + g * NSA_HEAD_DIM:sw + (g + 1) * NSA_HEAD_DIM]
        vw_g = v[:, 2 * sw + g * NSA_HEAD_DIM:2 * sw + (g + 1) * NSA_HEAD_DIM]
        vs_ref[g] = jnp.concatenate([vs_g, ones_col], axis=1).astype(BF16)
        vw_ref[g] = jnp.concatenate([vw_g, ones_col], axis=1).astype(BF16)
    gt_ref[...] = jax.nn.sigmoid(_dot(h, wg_ref[...]))


def _nsa_proj(x3, g_mix, w_in, q_gain, k_gain, rope, tm=512):
    b, s, d = x3.shape
    hh, gg, dh = NSA_HEADS, NSA_KV_GROUPS, NSA_HEAD_DIM
    qd, gw = hh * dh, gg * dh
    tm = min(tm, s)
    wq = w_in[:, :qd]
    wkv = w_in[:, qd:qd + 6 * gw].reshape(d, 3, 2, gw)
    wgate = jnp.pad(w_in[:, qd + 6 * gw:], ((0, 0), (0, LANES - 3 * hh)))

    def spread(a):
        a = a.reshape(a.shape[:-1] + (gg, dh))
        return jnp.pad(a, [(0, 0)] * (a.ndim - 1) + [(0, LANES - dh)]).reshape(a.shape[:-2] + (gg * LANES,))

    w_norm = jnp.concatenate([wq, wkv[:, 0, 0], wkv[:, 2, 0], spread(wkv[:, 1, 0])], axis=1).astype(BF16)
    w_v = jnp.concatenate([wkv[:, 0, 1], wkv[:, 1, 1], wkv[:, 2, 1]], axis=1).astype(BF16)
    gain = jnp.concatenate([jnp.tile(q_gain, hh) * (dh ** -0.5 * LOG2_E), jnp.tile(k_gain[0], gg), jnp.tile(k_gain[2], gg),
                            spread(jnp.tile(k_gain[1], gg))]).reshape(1, -1).astype(F32)
    lane = np.arange(2 * LANES)
    bd = np.stack([(lane[:, None] // dh == lane[None, :] // dh), (lane[:, None] // LANES == lane[None, :] // LANES)])
    bd = jnp.asarray(bd.astype(np.float32), BF16)
    n_norm = NSA_NORM_SECTIONS * 2 * LANES
    rope_spec = pl.BlockSpec((tm, LANES), lambda i, j: (j, 0))
    tok = lambda w: pl.BlockSpec((None, tm, w), lambda i, j: (i, j, 0))
    grp = pl.BlockSpec((None, gg, tm, dh), lambda i, j: (i, 0, j, 0))
    grp_v = pl.BlockSpec((None, gg, tm, LANES), lambda i, j: (i, 0, j, 0))
    st = NSA_CMP_STRIDE
    cmp_in = pl.BlockSpec((None, tm // st, st * gw), lambda i, j: (i, j, 0))
    sds = jax.ShapeDtypeStruct
    return pl.pallas_call(
        _nsa_proj_body,
        grid=(b, s // tm),
        in_specs=[tok(d), _const_spec((1, d)), _const_spec((d, n_norm)), _const_spec((d, 3 * gw)),
                  _const_spec((d, LANES)), _const_spec((1, n_norm)), _const_spec((2, 2 * LANES, 2 * LANES)),
                  rope_spec, rope_spec, rope_spec],
        out_specs=[tok(qd), cmp_in, grp, tok(gg * LANES), cmp_in, grp_v, grp_v, tok(LANES)],
        out_shape=[sds((b, s, qd), BF16), sds((b, s // st, st * gw), BF16), sds((b, gg, s, dh), BF16),
                   sds((b, s, gg * LANES), BF16), sds((b, s // st, st * gw), BF16), sds((b, gg, s, LANES), BF16),
                   sds((b, gg, s, LANES), BF16), sds((b, s, LANES), F32)],
        scratch_shapes=[pltpu.VMEM((gw // LANES, tm, LANES), F32)],
        compiler_params=_cp("parallel", "parallel"),
        name="nsa_proj",
    )(x3, _row(g_mix), w_norm, w_v, wgate.astype(BF16), gain, bd, *rope)


def _nsa_compress_body(kin_ref, vin_ref, wa_ref, wb_ref, pa_ref, pb_ref, b1_ref, w2_ref, k_ref, vt_ref, q_sc, *, n_cmp):
    nr = kin_ref.shape[0]
    dh = NSA_HEAD_DIM
    row = lax.broadcasted_iota(jnp.int32, (nr, 1), 0)
    for c, in_ref in enumerate((kin_ref, vin_ref)):
        x = in_ref[...].astype(F32)
        first = _dot((x + pa_ref[c]).astype(BF16), wa_ref[c])
        q_sc[0:nr, :] = _dot((x + pb_ref[c]).astype(BF16), wb_ref[c])
        q_sc[nr:nr + SUBLANES, :] = jnp.zeros((SUBLANES, q_sc.shape[1]), F32)
        hid = jax.nn.gelu(first + q_sc[1:nr + 1, :] + b1_ref[c])
        comp = jnp.where(row < n_cmp, _dot(hid.astype(BF16), w2_ref[c]), 0.0)
        if c == 0:
            for g in range(NSA_KV_GROUPS):
                k_ref[g] = comp[:, g * dh:(g + 1) * dh].astype(BF16)
        else:
            comp_t = comp.T
            for g in range(NSA_KV_GROUPS):
                vt_ref[g] = comp_t[g * dh:(g + 1) * dh, :].astype(BF16)


def _nsa_compress(kc16, vc16, cmp_pos, cmp_w1, cmp_b1, cmp_w2):
    b, nr, _ = kc16.shape
    gg, dh, st = NSA_KV_GROUPS, NSA_HEAD_DIM, NSA_CMP_STRIDE
    gw = gg * dh
    s = nr * st
    n_cmp = (s - NSA_CMP_BLOCK) // st + 1
    eye = jnp.eye(gg, dtype=F32)
    w1 = cmp_w1.reshape(2, 2, st, dh, dh)
    wexp = jnp.einsum('chjde,gf->chjgdfe', w1, eye).reshape(2, 2, st * gw, gw).astype(BF16)
    pos = jnp.broadcast_to(cmp_pos.reshape(2, 2, st, 1, dh), (2, 2, st, gg, dh)).reshape(2, 2, 1, st * gw).astype(F32)
    b1 = jnp.tile(cmp_b1, (1, gg)).reshape(2, 1, gw).astype(F32)
    w2 = jnp.einsum('cde,gf->cgdfe', cmp_w2, eye).reshape(2, gw, gw).astype(BF16)
    full = lambda shape: _const_spec(shape)
    return pl.pallas_call(
        functools.partial(_nsa_compress_body, n_cmp=n_cmp),
        grid=(b,),
        in_specs=[pl.BlockSpec((None, nr, st * gw), lambda i: (i, 0, 0)),
                  pl.BlockSpec((None, nr, st * gw), lambda i: (i, 0, 0)),
                  full((2, st * gw, gw)), full((2, st * gw, gw)), full((2, 1, st * gw)), full((2, 1, st * gw)),
                  full((2, 1, gw)), full((2, gw, gw))],
        out_specs=[pl.BlockSpec((None, gg, nr, dh), lambda i: (i, 0, 0, 0)),
                   pl.BlockSpec((None, gg, dh, nr), lambda i: (i, 0, 0, 0))],
        out_shape=[jax.ShapeDtypeStruct((b, gg, nr, dh), BF16), jax.ShapeDtypeStruct((b, gg, dh, nr), BF16)],
        scratch_shapes=[pltpu.VMEM((nr + SUBLANES, gw), F32)],
        compiler_params=_cp("parallel"),
        name="nsa_compress",
    )(kc16, vc16, wexp[:, 0], wexp[:, 1], pos[:, 0], pos[:, 1], b1, w2)


def _stacked_row_index(n_rep, tq):
    assert tq & (tq - 1) == 0
    return jnp.bitwise_and(lax.broadcasted_iota(jnp.int32, (n_rep * tq, 1), 0), tq - 1)


def _nt_dot(a, b):
    return lax.dot_general(a, b, (((1,), (1,)), ((), ())), preferred_element_type=F32)


def _nsa_cmp_select_body(q_ref, kc_ref, vt_ref, ovl_ref, oc_ref, ns_ref, sc_sc, *, n_levels):
    tq = q_ref.shape[0]
    nr = kc_ref.shape[0]
    dh = NSA_HEAD_DIM
    nb = NSA_MAX_SEL_BLOCKS
    q0 = pl.program_id(2) * tq
    t_lane = q0 + lax.broadcasted_iota(jnp.int32, (1, tq), 1)
    sub = lax.broadcasted_iota(jnp.int32, (SUBLANES, 1), 0)
    span = SUBLANES * NSA_SLC_BLOCK
    level = (q0 + tq - 1) // span

    def at_level(lv):
        n_tiles = lv + 1
        nbl = n_tiles * SUBLANES
        nrl = min(nr, -(-(n_tiles * span // NSA_CMP_STRIDE) // LANES) * LANES)
        kc, vt = kc_ref[0:nrl, :], vt_ref[:, 0:nrl]
        end_row = lax.broadcasted_iota(jnp.int32, (nrl, 1), 0) * NSA_CMP_STRIDE + (NSA_CMP_BLOCK - 1)
        mask_t = end_row <= t_lane
        psum_t = jnp.zeros((nrl, tq), F32)
        outs_t = []
        for h in range(NSA_REP):
            qh = q_ref[:, h * dh:(h + 1) * dh]
            s_t = jnp.where(mask_t, _nt_dot(kc, qh), NEG_INF)
            m_t = jnp.maximum(jnp.max(s_t, axis=0, keepdims=True), 0.1 * NEG_INF)
            e_t = jnp.exp2(s_t - m_t)
            den_t = jnp.sum(e_t, axis=0, keepdims=True)
            p_t = e_t * (1.0 / jnp.where(den_t > 0.0, den_t, 1.0))
            psum_t = psum_t + p_t
            outs_t.append(_dot(vt, p_t.astype(BF16)))
        oc_ref[...] = jnp.concatenate(outs_t, axis=0).T.astype(BF16)

        p_hi = psum_t.astype(BF16)
        p_lo = (psum_t - p_hi.astype(F32)).astype(BF16)
        ovl = ovl_ref[0:-(-nbl // 16) * 16, 0:nrl]
        imp = (_dot(ovl, p_hi) + _dot(ovl, p_lo))[0:nbl]
        blk = lax.broadcasted_iota(jnp.int32, (nbl, 1), 0)
        cur = t_lane // NSA_SLC_BLOCK
        valid = blk <= cur
        forced = (blk == 0) | (blk >= cur - (NSA_LOCAL_BLOCKS - 1))
        score = jnp.where(valid, jnp.where(forced, BIG, imp), -BIG)
        sc_sc[0:nbl, :] = score
        tiles = [score[a * SUBLANES:(a + 1) * SUBLANES] for a in range(n_tiles)]
        ranks = [jnp.zeros((SUBLANES, tq), F32) for _ in range(n_tiles)]
        for j in range(nbl):
            sj = sc_sc[j:j + 1, :]
            jt = j // SUBLANES
            for a in range(n_tiles):
                if a > jt:
                    ahead = jnp.where(sj >= tiles[a], 1.0, 0.0)
                elif a < jt:
                    ahead = jnp.where(sj > tiles[a], 1.0, 0.0)
                else:
                    ahead = jnp.where(sub > j - a * SUBLANES,
                                      jnp.where(sj >= tiles[a], 1.0, 0.0), jnp.where(sj > tiles[a], 1.0, 0.0))
                ranks[a] = ranks[a] + ahead
        notsel = [jnp.where(r < float(NSA_TOP_N), 0.0, 1.0) for r in ranks]
        notsel = jnp.concatenate(notsel + [jnp.ones((LANES - nbl, tq), F32)], axis=0)
        ns_ref[...] = notsel.T[:, :nb].astype(BF16)

    for lv in range(n_levels):
        pl.when(level == lv)(functools.partial(at_level, lv))


def _nsa_cmp_select(q, k_cmp, vt_cmp, tq=256):
    b, s, qd = q.shape
    gg, dh = NSA_KV_GROUPS, NSA_HEAD_DIM
    nr = k_cmp.shape[2]
    nb = NSA_MAX_SEL_BLOCKS
    tq = min(tq, s)
    assert s // NSA_SLC_BLOCK <= nb
    c_start = np.arange(nr) * NSA_CMP_STRIDE
    c_end = c_start + NSA_CMP_BLOCK - 1
    s_start = np.arange(nb) * NSA_SLC_BLOCK
    s_end = s_start + NSA_SLC_BLOCK - 1
    ovl = ((c_start[None, :] <= s_end[:, None]) & (c_end[None, :] >= s_start[:, None])).astype(np.float32)
    n_levels = -(-s // (SUBLANES * NSA_SLC_BLOCK))
    return pl.pallas_call(
        functools.partial(_nsa_cmp_select_body, n_levels=n_levels),
        grid=(b, gg, s // tq),
        in_specs=[pl.BlockSpec((None, tq, NSA_REP * dh), lambda i, g, j: (i, j, g)),
                  pl.BlockSpec((None, None, nr, dh), lambda i, g, j: (i, g, 0, 0)),
                  pl.BlockSpec((None, None, dh, nr), lambda i, g, j: (i, g, 0, 0)),
                  _const_spec((nb, nr))],
        out_specs=[pl.BlockSpec((None, tq, NSA_REP * dh), lambda i, g, j: (i, j, g)),
                   pl.BlockSpec((None, None, tq, nb), lambda i, g, j: (i, g, j, 0))],
        out_shape=[jax.ShapeDtypeStruct((b, s, qd), BF16), jax.ShapeDtypeStruct((b, gg, s, nb), BF16)],
        scratch_shapes=[pltpu.VMEM((nb, tq), F32)],
        compiler_params=_cp("parallel", "parallel", "parallel"),
        name="nsa_cmp_select",
    )(q, k_cmp, vt_cmp, jnp.asarray(ovl, BF16))


def _nsa_sel_body(q_ref, ns_ref, k_ref, v_ref, o_ref, qa_sc, s_sc, m_sc, acc_sc):
    tq = q_ref.shape[0]
    dh = NSA_HEAD_DIM
    qi = pl.program_id(2)
    for h in range(NSA_REP):
        qa_sc[h * tq:(h + 1) * tq, :] = jnp.concatenate([q_ref[:, h * dh:(h + 1) * dh], ns_ref[...]], axis=1)
    m_sc[...] = jnp.full(m_sc.shape, NEG_INF, F32)
    acc_sc[...] = jnp.zeros(acc_sc.shape, F32)
    t_pos = lax.broadcasted_iota(jnp.int32, (tq, 1), 0)
    k_pos = lax.broadcasted_iota(jnp.int32, (1, tq), 1)

    def scores(kt):
        start = pl.multiple_of(kt * tq, tq)
        s_sc[...] = _nt_dot(qa_sc[...], k_ref[pl.ds(start, tq), :])

    def attend(kt, diagonal):
        start = pl.multiple_of(kt * tq, tq)
        v = v_ref[pl.ds(start, tq), :]
        for h in range(NSA_REP):
            sh = s_sc[pl.ds(h * tq, tq), :]
            if diagonal:
                sh = jnp.where(k_pos <= t_pos, sh, NEG_INF)
            m_old = m_sc[h]
            m_new = jnp.maximum(m_old, jnp.max(sh, axis=-1, keepdims=True))
            p = jnp.exp2(sh - jnp.concatenate([m_new] * (tq // LANES), axis=1))
            acc_sc[h] = jnp.exp2(m_old - m_new) * acc_sc[h] + _dot(p.astype(BF16), v)
            m_sc[h] = m_new

    def full_tile(kt, carry):
        attend(kt, False)
        scores(kt + 1)
        return carry

    scores(0)
    lax.fori_loop(0, qi, full_tile, 0)
    attend(qi, True)
    outs = []
    for h in range(NSA_REP):
        acc = acc_sc[h]
        outs.append(acc[:, :dh] / acc[:, dh:dh + 1])
    o_ref[...] = jnp.concatenate(outs, axis=1).astype(BF16)


def _nsa_selected(q, notsel, ksa, vs, tq=256):
    b, s, qd = q.shape
    gg, dh, rr = NSA_KV_GROUPS, NSA_HEAD_DIM, NSA_REP
    tq = min(tq, s)
    assert tq % LANES == 0
    return pl.pallas_call(
        _nsa_sel_body,
        grid=(b, gg, s // tq),
        in_specs=[pl.BlockSpec((None, tq, rr * dh), lambda i, g, j: (i, j, g)),
                  pl.BlockSpec((None, None, tq, NSA_MAX_SEL_BLOCKS), lambda i, g, j: (i, g, j, 0)),
                  pl.BlockSpec((None, s, LANES), lambda i, g, j: (i, 0, g)),
                  pl.BlockSpec((None, None, s, LANES), lambda i, g, j: (i, g, 0, 0))],
        out_specs=pl.BlockSpec((None, tq, rr * dh), lambda i, g, j: (i, j, g)),
        out_shape=jax.ShapeDtypeStruct((b, s, qd), BF16),
        scratch_shapes=[pltpu.VMEM((rr * tq, LANES), BF16), pltpu.VMEM((rr * tq, tq), F32),
                        pltpu.VMEM((rr, tq, LANES), F32), pltpu.VMEM((rr, tq, LANES), F32)],
        compiler_params=_cp("parallel", "parallel", "arbitrary"),
        name="nsa_selected",
    )(q, notsel, ksa, vs)


def _nsa_window_body(q_ref, *refs, n_kv):
    k_refs, v_refs, o_ref = refs[:n_kv], refs[n_kv:2 * n_kv], refs[2 * n_kv]
    tq = q_ref.shape[0]
    dh = NSA_HEAD_DIM
    qi = pl.program_id(2)
    qs = jnp.concatenate([q_ref[:, h * dh:(h + 1) * dh] for h in range(NSA_REP)], axis=0)
    t_pos = lax.broadcasted_iota(jnp.int32, (tq, 1), 0)
    k_rel = lax.broadcasted_iota(jnp.int32, (1, tq), 1)
    scores = [_nt_dot(qs, k_refs[j][...]) for j in range(n_kv)]
    outs = []
    for h in range(NSA_REP):
        parts = []
        for j in range(n_kv):
            sh = scores[j][h * tq:(h + 1) * tq]
            back = (n_kv - 1 - j) * tq
            in_seq = qi >= n_kv - 1 - j
            if j == n_kv - 1:
                parts.append(jnp.where(k_rel <= t_pos, sh, NEG_INF))
            elif j == 0:
                limit = jnp.where(in_seq, NSA_WINDOW, -(2 ** 30))
                parts.append(jnp.where(t_pos - k_rel + back < limit, sh, NEG_INF))
            else:
                parts.append(sh + jnp.where(in_seq, 0.0, NEG_INF))
        m = functools.reduce(jnp.maximum, [jnp.max(p_, axis=-1, keepdims=True) for p_ in parts])
        parts = [jnp.exp2(p_ - m) for p_ in parts]
        o = functools.reduce(jnp.add, [_dot(parts[j].astype(BF16), v_refs[j][...]) for j in range(n_kv)])
        outs.append(o[:, :dh] / o[:, dh:dh + 1])
    o_ref[...] = jnp.concatenate(outs, axis=1).astype(BF16)


def _nsa_window(q, kw, vw, tq=256):
    b, s, qd = q.shape
    gg, dh, rr = NSA_KV_GROUPS, NSA_HEAD_DIM, NSA_REP
    tq = min(tq, s)
    assert NSA_WINDOW % tq == 0
    n_kv = NSA_WINDOW // tq + 1
    kv_spec = lambda off, w: pl.BlockSpec((None, None, tq, w), lambda i, g, j: (i, g, jnp.maximum(j - off, 0), 0))
    k_specs = [kv_spec(n_kv - 1 - j, dh) for j in range(n_kv)]
    v_specs = [kv_spec(n_kv - 1 - j, LANES) for j in range(n_kv)]
    return pl.pallas_call(
        functools.partial(_nsa_window_body, n_kv=n_kv),
        grid=(b, gg, s // tq),
        in_specs=[pl.BlockSpec((None, tq, rr * dh), lambda i, g, j: (i, j, g))] + k_specs + v_specs,
        out_specs=pl.BlockSpec((None, tq, rr * dh), lambda i, g, j: (i, j, g)),
        out_shape=jax.ShapeDtypeStruct((b, s, qd), BF16),
        compiler_params=_cp("parallel", "parallel", "parallel"),
        name="nsa_window",
    )(q, *([kw] * n_kv), *([vw] * n_kv))


def _nsa_out_body(x_ref, oc_ref, os_ref, ow_ref, gt_ref, ex_ref, w_ref, o_ref):
    qd = oc_ref.shape[1]
    gt = gt_ref[...]
    g_hi = gt.astype(BF16)
    g_lo = (gt - g_hi.astype(F32)).astype(BF16)
    o = jnp.zeros(oc_ref.shape, F32)
    for br, ref in enumerate((oc_ref, os_ref, ow_ref)):
        ex = ex_ref[:, br * qd:(br + 1) * qd]
        o = o + (_dot(g_hi, ex) + _dot(g_lo, ex)) * ref[...].astype(F32)
    o_ref[...] = x_ref[...] + _dot(o.astype(BF16), w_ref[...])


def _nsa_out(x2, oc, osel, ow, gates, w_out, tm=512):
    t, d = x2.shape
    qd = oc.shape[1]
    tm = min(tm, t)
    col = np.arange(3 * qd)
    expand = (np.arange(LANES)[:, None] == (col // qd) * NSA_HEADS + (col % qd) // NSA_HEAD_DIM).astype(np.float32)
    tok = lambda w: pl.BlockSpec((tm, w), lambda i: (i, 0))
    return pl.pallas_call(
        _nsa_out_body,
        grid=(t // tm,),
        in_specs=[tok(d), tok(qd), tok(qd), tok(qd), tok(LANES), _const_spec((LANES, 3 * qd)), _const_spec((qd, d))],
        out_specs=tok(d),
        out_shape=jax.ShapeDtypeStruct((t, d), F32),
        compiler_params=_cp("parallel"),
        name="nsa_out",
    )(x2, oc, osel, ow, gates, jnp.asarray(expand, BF16), w_out.astype(BF16))


def _nsa_layer(x3, positions, g_mix, w_in, q_gain, k_gain, cmp_pos, cmp_w1, cmp_b1, cmp_w2, w_out):
    b, s, d = x3.shape
    rope = _rope_tables(positions)
    q, kc, kw, ksa, vc, vs, vw, gates = _nsa_proj(x3, g_mix, w_in, q_gain, k_gain, rope)
    k_cmp, vt_cmp = _nsa_compress(kc, vc, cmp_pos, cmp_w1, cmp_b1, cmp_w2)
    oc, notsel = _nsa_cmp_select(q, k_cmp, vt_cmp)
    osel = _nsa_selected(q, notsel, ksa, vs)
    ow = _nsa_window(q, kw, vw)
    t = b * s
    flat = lambda a: a.reshape(t, a.shape[-1])
    return _nsa_out(x3.reshape(t, d), flat(oc), flat(osel), flat(ow), flat(gates), w_out).reshape(b, s, d)


def kernel(x, positions, norm_mix, norm_ffn, ffn_w_in, ffn_w_out, conv_w_in, conv_b_in, conv_w_dw, conv_b_dw, conv_ln_g, conv_ln_b, conv_w_out, nsa_w_in, nsa_q_gain, nsa_k_gain, nsa_cmp_pos, nsa_cmp_w1, nsa_cmp_b1, nsa_cmp_w2, nsa_w_out, s5_lam_re, s5_lam_im, s5_log_step, s5_b_re, s5_b_im, s5_c_re, s5_c_im, s5_d, s5_w_glu, s5_b_glu, pool_w, pool_scale):
    b, s, d = x.shape
    depth = norm_mix.shape[0]
    for i in range(depth):
        m, j = i % N_MIXERS, i // N_MIXERS
        if m == 0:
            x = _conv_layer(x, norm_mix[i], conv_w_in[j], conv_b_in[j], conv_w_dw[j], conv_b_dw[j],
                            conv_ln_g[j], conv_ln_b[j], conv_w_out[j])
        elif m == 1:
            x = _nsa_layer(x, positions, norm_mix[i], nsa_w_in[j], nsa_q_gain[j], nsa_k_gain[j], nsa_cmp_pos[j],
                           nsa_cmp_w1[j], nsa_cmp_b1[j], nsa_cmp_w2[j], nsa_w_out[j])
        elif m == 2:
            x = _s5_layer(x, norm_mix[i], s5_lam_re[j], s5_lam_im[j], s5_log_step[j], s5_b_re[j], s5_b_im[j],
                          s5_c_re[j], s5_c_im[j], s5_d[j], s5_w_glu[j], s5_b_glu[j])
        elif m == 3:
            x = _pool_layer(x, norm_mix[i], pool_w[j], pool_scale[j])
        x = _ffn(x.reshape(b * s, d), norm_ffn[i], ffn_w_in[i], ffn_w_out[i]).reshape(b, s, d)
    return x
```

```python
import functools
import math

import numpy as np
import jax
import jax.numpy as jnp
from jax import lax
from jax.experimental import pallas as pl
from jax.experimental.pallas import tpu as pltpu

F32 = jnp.float32
BF16 = jnp.bfloat16

RMS_EPS = 1e-6
LN_EPS = 1e-5
N_MIXERS = 4
CONV_WIDTH = 31
POOL_WINDOWS = (2, 4, 8, 16)

V7X_VMEM_LIMIT_BYTES = 56 * 1024 * 1024
SUBLANES = 8
LANES = 128


def _cp(*sem):
    return pltpu.CompilerParams(dimension_semantics=sem, vmem_limit_bytes=V7X_VMEM_LIMIT_BYTES)


def _const_spec(shape):
    nd = len(shape)
    return pl.BlockSpec(shape, lambda *_: (0,) * nd, pipeline_mode=pl.Buffered(1))


def _dot(a, b):
    return jnp.dot(a, b, preferred_element_type=F32)


def _rms(x, g):
    ms = jnp.mean(x * x, axis=-1, keepdims=True)
    return x * lax.rsqrt(ms + RMS_EPS) * g


def _row(v):
    return v.reshape(1, -1).astype(F32)


FFN_CHUNK = 256


def _ffn_body(x_ref, g_ref, wg_ref, wu_ref, wo_ref, o_ref, act_sc):
    x = x_ref[...]
    h = _rms(x, g_ref[...]).astype(BF16)
    n_chunks = act_sc.shape[1] // FFN_CHUNK
    for c in range(n_chunks):
        sl = slice(c * FFN_CHUNK, (c + 1) * FFN_CHUNK)
        gate = _dot(h, wg_ref[:, sl])
        up = _dot(h, wu_ref[:, sl])
        act_sc[:, sl] = (gate * jax.nn.sigmoid(gate) * up).astype(BF16)
    o_ref[...] = x + _dot(act_sc[...], wo_ref[...])


def _ffn(x2, g, w_in, w_out, tm=512):
    t, d = x2.shape
    f = w_out.shape[0]
    w_in = w_in.astype(BF16)
    w_out = w_out.astype(BF16)
    return pl.pallas_call(
        _ffn_body,
        grid=(t // tm,),
        in_specs=[
            pl.BlockSpec((tm, d), lambda i: (i, 0)),
            _const_spec((1, d)),
            pl.BlockSpec((d, f), lambda i: (0, 0), pipeline_mode=pl.Buffered(1)),
            pl.BlockSpec((d, f), lambda i: (0, 1), pipeline_mode=pl.Buffered(1)),
            _const_spec((f, d)),
        ],
        out_specs=pl.BlockSpec((tm, d), lambda i: (i, 0)),
        out_shape=jax.ShapeDtypeStruct((t, d), F32),
        scratch_shapes=[pltpu.VMEM((tm, f), BF16)],
        compiler_params=_cp("parallel"),
        name="ffn",
    )(x2, _row(g), w_in, w_in, w_out)


CONV_HALO = 32
CONV_ROWS = 64


def _conv_body(x_ref, g_ref, wa_ref, wg_ref, ba_ref, bg_ref, wdw_ref, bdw_ref,
               lng_ref, lnb_ref, wo_ref, o_ref, u_sc, v_sc):
    tm, d = x_ref.shape
    s = pl.program_id(1)

    @pl.when(s == 0)
    def _():
        u_sc[0:CONV_HALO, :] = jnp.zeros((CONV_HALO, d), F32)

    @pl.when(s > 0)
    def _():
        u_sc[0:CONV_HALO, :] = u_sc[tm:tm + CONV_HALO, :]

    x = x_ref[...]
    h = _rms(x, g_ref[...]).astype(BF16)
    a = _dot(h, wa_ref[...]) + ba_ref[...]
    gt = _dot(h, wg_ref[...]) + bg_ref[...]
    u_sc[CONV_HALO:CONV_HALO + tm, :] = a * jax.nn.sigmoid(gt)

    off = CONV_HALO - (CONV_WIDTH - 1)

    def conv_rows(c, carry):
        r0 = pl.multiple_of(c * CONV_ROWS, CONV_ROWS)
        for lt in range(d // LANES):
            lanes = slice(lt * LANES, (lt + 1) * LANES)
            win = u_sc[pl.ds(r0, CONV_ROWS + CONV_HALO), lanes]
            acc = jnp.zeros((CONV_ROWS, LANES), F32) + bdw_ref[:, lanes]
            for sft in range(SUBLANES):
                shifted = pltpu.roll(win, CONV_ROWS + CONV_HALO - sft, axis=0) if sft else win
                for j in range(sft, CONV_HALO + 1, SUBLANES):
                    k = j - off
                    if 0 <= k < CONV_WIDTH:
                        acc = acc + wdw_ref[k:k + 1, lanes] * shifted[j - sft:j - sft + CONV_ROWS]
            v_sc[pl.ds(r0, CONV_ROWS), lanes] = acc
        return carry

    lax.fori_loop(0, tm // CONV_ROWS, conv_rows, 0)

    v = v_sc[...]
    mu = jnp.mean(v, axis=-1, keepdims=True)
    vc = v - mu
    var = jnp.mean(vc * vc, axis=-1, keepdims=True)
    y = vc * lax.rsqrt(var + LN_EPS) * lng_ref[...] + lnb_ref[...]
    y = (y * jax.nn.sigmoid(y)).astype(BF16)
    o_ref[...] = x + _dot(y, wo_ref[...])


def _conv_layer(x3, g, w_in, b_in, w_dw, b_dw, ln_g, ln_b, w_out, tm=256):
    b, s, d = x3.shape
    w_in = w_in.astype(BF16)
    return pl.pallas_call(
        _conv_body,
        grid=(b, s // tm),
        in_specs=[
            pl.BlockSpec((None, tm, d), lambda i, j: (i, j, 0)),
            _const_spec((1, d)),
            pl.BlockSpec((d, d), lambda i, j: (0, 0), pipeline_mode=pl.Buffered(1)),
            pl.BlockSpec((d, d), lambda i, j: (0, 1), pipeline_mode=pl.Buffered(1)),
            pl.BlockSpec((1, d), lambda i, j: (0, 0), pipeline_mode=pl.Buffered(1)),
            pl.BlockSpec((1, d), lambda i, j: (0, 1), pipeline_mode=pl.Buffered(1)),
            _const_spec((CONV_WIDTH, d)),
            _const_spec((1, d)),
            _const_spec((1, d)),
            _const_spec((1, d)),
            _const_spec((d, d)),
        ],
        out_specs=pl.BlockSpec((None, tm, d), lambda i, j: (i, j, 0)),
        out_shape=jax.ShapeDtypeStruct((b, s, d), F32),
        scratch_shapes=[pltpu.VMEM((CONV_HALO + tm, d), F32), pltpu.VMEM((tm, d), F32)],
        compiler_params=_cp("parallel", "arbitrary"),
        name="conv_module",
    )(x3, _row(g), w_in, w_in, _row(b_in), _row(b_in), w_dw.astype(F32), _row(b_dw),
      _row(ln_g), _row(ln_b), w_out.astype(BF16))


POOL_HALO = 16


def _pool_body(x_ref, g_ref, w_ref, sc_ref, o_ref, h_sc):
    tm, d = x_ref.shape
    s = pl.program_id(1)
    grp = d // len(POOL_WINDOWS)

    @pl.when(s == 0)
    def _():
        h_sc[0:POOL_HALO, :] = jnp.zeros((POOL_HALO, d), F32)

    @pl.when(s > 0)
    def _():
        h_sc[0:POOL_HALO, :] = h_sc[tm:tm + POOL_HALO, :]

    x = x_ref[...]
    h_sc[POOL_HALO:POOL_HALO + tm, :] = _rms(x, g_ref[...])
    t = s * tm + lax.broadcasted_iota(jnp.int32, (tm, 1), 0)
    for gi, win in enumerate(POOL_WINDOWS):
        lo = gi * grp
        hcur = h_sc[POOL_HALO:POOL_HALO + tm, lo:lo + grp]
        tot = hcur
        for k in range(1, win):
            tot = tot + h_sc[POOL_HALO - k:POOL_HALO - k + tm, lo:lo + grp]
        cnt = jnp.minimum(t + 1, win).astype(F32)
        pooled = (tot * (1.0 / cnt) - hcur).astype(BF16)
        y = _dot(pooled, w_ref[gi]) * sc_ref[:, lo:lo + grp]
        o_ref[:, lo:lo + grp] = x[:, lo:lo + grp] + y


def _pool_layer(x3, g, w_grp, scale, tm=512):
    b, s, d = x3.shape
    ng, grp, _ = w_grp.shape
    return pl.pallas_call(
        _pool_body,
        grid=(b, s // tm),
        in_specs=[
            pl.BlockSpec((None, tm, d), lambda i, j: (i, j, 0)),
            _const_spec((1, d)),
            _const_spec((ng, grp, grp)),
            _const_spec((1, d)),
        ],
        out_specs=pl.BlockSpec((None, tm, d), lambda i, j: (i, j, 0)),
        out_shape=jax.ShapeDtypeStruct((b, s, d), F32),
        scratch_shapes=[pltpu.VMEM((POOL_HALO + tm, d), F32)],
        compiler_params=_cp("parallel", "arbitrary"),
        name="pool",
    )(x3, _row(g), w_grp.astype(BF16), _row(scale))


S5_L = 8
S5_GPT = 8


def _s5_operators(lam_re, lam_im, log_step, b_re, b_im, c_re, c_im):
    hi = lax.Precision.HIGHEST
    n_g, n_p = lam_re.shape
    n_c = b_re.shape[-1]
    n_j = n_g // S5_GPT
    L = S5_L
    def cmul(a, b):
        return a[0] * b[0] - a[1] * b[1], a[0] * b[1] + a[1] * b[0]

    def cexp(z):
        mag = jnp.exp(z[0])
        return mag * jnp.cos(z[1]), mag * jnp.sin(z[1])

    lr, li = lam_re.astype(F32), lam_im.astype(F32)
    step = jnp.exp(log_step.astype(F32))[:, None]
    lam_bar = cexp((lr * step, li * step))
    inv_den = 1.0 / (lr * lr + li * li)
    zoh = cmul((lam_bar[0] - 1.0, lam_bar[1]), (lr * inv_den, -li * inv_den))
    b_bar = cmul((zoh[0][:, :, None], zoh[1][:, :, None]), (b_re.astype(F32), b_im.astype(F32)))
    c_mat = (c_re.astype(F32), c_im.astype(F32))
    dd = jnp.arange(2 * L, dtype=F32)[:, None, None]
    pw = cexp(((lr * step)[None] * dd, (li * step)[None] * dd))

    pb = cmul((pw[0][:L, :, :, None], pw[1][:L, :, :, None]), (b_bar[0][None], b_bar[1][None]))
    kern = (jnp.einsum('gap,dgpc->gdac', c_mat[0], pb[0], precision=hi)
            - jnp.einsum('gap,dgpc->gdac', c_mat[1], pb[1], precision=hi))
    sig = np.arange(L)[:, None]
    tau = np.arange(L)[None, :]
    lag = np.clip(tau - sig, 0, L - 1)
    kt = kern[:, lag] * jnp.asarray((tau >= sig).astype(np.float32))[None, :, :, None, None]
    kt = kt.reshape(n_j, S5_GPT, L, L, n_c, n_c)
    m_op = kt.transpose(0, 2, 1, 5, 3, 4).reshape(n_j, L * S5_GPT * n_c, L * n_c)

    e = jnp.stack(pb, axis=0)[:, ::-1].transpose(0, 2, 1, 3, 4)
    e = e.reshape(2, n_j, S5_GPT, L, n_p, n_c)
    b_op = e.transpose(1, 3, 2, 5, 0, 4).reshape(n_j, L * S5_GPT * n_c, 2 * n_p)

    f = cmul((c_mat[0][:, None], c_mat[1][:, None]),
             (pw[0][1:L + 1].transpose(1, 0, 2)[:, :, None, :], pw[1][1:L + 1].transpose(1, 0, 2)[:, :, None, :]))
    f = jnp.stack([f[0], -f[1]], axis=0).reshape(2, n_j, S5_GPT, L, n_c, n_p)
    c_op = f.transpose(1, 0, 2, 5, 3, 4).reshape(n_j, 2 * S5_GPT * n_p, L * n_c)

    ak = cexp(((lr * step * L)[None] * dd, (li * step * L)[None] * dd))
    ak = jnp.stack(ak, axis=1).reshape(2 * L, 2, n_j, S5_GPT * n_p)
    a_pw = ak.transpose(2, 0, 1, 3).reshape(n_j, 2 * L, 2 * S5_GPT * n_p)
    return m_op.astype(BF16), b_op.astype(BF16), c_op.astype(BF16), a_pw.astype(F32)


def _s5_regroup_body(x_ref, g_ref, o_ref, u_sc):
    tm, d = x_ref.shape
    u = _rms(x_ref[...], g_ref[...])
    rows = tm // S5_L
    for j in range(d // LANES):
        u_sc[j] = u[:, j * LANES:(j + 1) * LANES]
    for j in range(d // LANES):
        for sg in range(S5_L):
            o_ref[j, :, sg * LANES:(sg + 1) * LANES] = (
                u_sc[j, pl.ds(sg, rows, stride=S5_L), :].astype(BF16))


def _s5_regroup(x2, g, tm=512):
    t, d = x2.shape
    n_j = d // LANES
    return pl.pallas_call(
        _s5_regroup_body,
        grid=(t // tm,),
        in_specs=[pl.BlockSpec((tm, d), lambda i: (i, 0)), _const_spec((1, d))],
        out_specs=pl.BlockSpec((n_j, tm // S5_L, S5_L * LANES), lambda i: (0, i, 0)),
        out_shape=jax.ShapeDtypeStruct((n_j, t // S5_L, S5_L * LANES), BF16),
        scratch_shapes=[pltpu.VMEM((n_j, tm, LANES), F32)],
        compiler_params=_cp("parallel"),
        name="s5_regroup",
    )(x2, _row(g))


def _s5_chunk_body(u_ref, mc_ref, bc_ref, cc_ref, a_ref, rep_ref, y_ref,
                   m_ref, b_ref, c_ref, xr_sc, xi_sc, pr_sc, pi_sc, cr_sc, ci_sc):
    rows, width = u_ref.shape
    half = width // 2
    nt = pl.program_id(2)

    @pl.when((pl.program_id(1) == 0) & (nt == 0))
    def _():
        n_c = width // (S5_L * S5_GPT)
        n_p = half // S5_GPT
        cw = 2 * LANES
        for op_ref, src_ref, rep, row_div, col_div in (
                (m_ref, mc_ref, 0, n_c, n_c), (b_ref, bc_ref, 1, n_c, n_p), (c_ref, cc_ref, 0, n_p, n_c)):
            g_row = (lax.broadcasted_iota(jnp.int32, (width, 1), 0) // row_div) % S5_GPT
            for c0 in range(0, width, cw):
                g_col = ((c0 + lax.broadcasted_iota(jnp.int32, (1, cw), 1)) // col_div) % S5_GPT
                full = _dot(src_ref[...], rep_ref[rep, :, c0:c0 + cw])
                op_ref[:, c0:c0 + cw] = jnp.where(g_row == g_col, full, 0.0).astype(BF16)

    @pl.when(nt == 0)
    def _():
        cr_sc[...] = jnp.zeros_like(cr_sc)
        ci_sc[...] = jnp.zeros_like(ci_sc)

    u = u_ref[...]
    xin = _dot(u, b_ref[...])
    xr, xi = xin[:, :half], xin[:, half:]
    row = lax.broadcasted_iota(jnp.int32, (rows, 1), 0) % SUBLANES
    for shift in (1, 2, 4):
        ar, ai = a_ref[shift:shift + 1, :half], a_ref[shift:shift + 1, half:]
        keep = row >= shift
        sr = jnp.where(keep, pltpu.roll(xr, shift, axis=0), 0.0)
        si = jnp.where(keep, pltpu.roll(xi, shift, axis=0), 0.0)
        xr, xi = xr + ar * sr - ai * si, xi + ar * si + ai * sr
    xr_sc[...] = xr
    xi_sc[...] = xi
    keep = row >= 1
    pr_sc[...] = jnp.where(keep, pltpu.roll(xr, 1, axis=0), 0.0)
    pi_sc[...] = jnp.where(keep, pltpu.roll(xi, 1, axis=0), 0.0)

    akr, aki = a_ref[0:SUBLANES, :half], a_ref[0:SUBLANES, half:]
    a8r, a8i = a_ref[SUBLANES:SUBLANES + 1, :half], a_ref[SUBLANES:SUBLANES + 1, half:]

    def tile_step(t, carry):
        cr, ci = carry
        base = pl.multiple_of(t * SUBLANES, SUBLANES)
        pr_sc[pl.ds(base, SUBLANES), :] = pr_sc[pl.ds(base, SUBLANES), :] + akr * cr - aki * ci
        pi_sc[pl.ds(base, SUBLANES), :] = pi_sc[pl.ds(base, SUBLANES), :] + akr * ci + aki * cr
        lr = xr_sc[pl.ds(base + SUBLANES - 1, 1), :]
        li = xi_sc[pl.ds(base + SUBLANES - 1, 1), :]
        return lr + a8r * cr - a8i * ci, li + a8r * ci + a8i * cr

    cr, ci = lax.fori_loop(0, rows // SUBLANES, tile_step, (cr_sc[...], ci_sc[...]))
    cr_sc[...] = cr
    ci_sc[...] = ci

    y = _dot(u, m_ref[...])
    y = y + _dot(pr_sc[...].astype(BF16), c_ref[0:half, :])
    y = y + _dot(pi_sc[...].astype(BF16), c_ref[half:, :])
    y_ref[...] = y


def _s5_chunks(u3, m_op, b_op, c_op, a_pw, bsz):
    n_j, n_rows, width = u3.shape
    rows_per_seq = n_rows // bsz
    rows = min(512, rows_per_seq)
    nt = rows_per_seq // rows
    half = width // 2
    narrow = m_op.shape[-1]
    col = np.arange(width)
    n_c, n_p = width // (S5_L * S5_GPT), half // S5_GPT
    src_ta = (col // (S5_GPT * n_c)) * n_c + col % n_c
    src_rp = (col // (S5_GPT * n_p)) * n_p + col % n_p
    rep = np.stack([np.arange(narrow)[:, None] == src_ta[None, :], np.arange(narrow)[:, None] == src_rp[None, :]])
    wspec = lambda shape: pl.BlockSpec((None,) + shape, lambda j, b, t: (j, 0, 0), pipeline_mode=pl.Buffered(1))
    return pl.pallas_call(
        _s5_chunk_body,
        grid=(n_j, bsz, nt),
        in_specs=[
            pl.BlockSpec((None, rows, width), lambda j, b, t: (j, b * nt + t, 0)),
            wspec((width, narrow)), wspec((width, narrow)), wspec((width, narrow)),
            wspec((2 * S5_L, width)), _const_spec((2, narrow, width)),
        ],
        out_specs=pl.BlockSpec((None, rows, width), lambda j, b, t: (j, b * nt + t, 0)),
        out_shape=jax.ShapeDtypeStruct((n_j, n_rows, width), F32),
        scratch_shapes=([pltpu.VMEM((width, width), BF16)] * 3 + [pltpu.VMEM((rows, half), F32)] * 4
                        + [pltpu.VMEM((1, half), F32)] * 2),
        compiler_params=_cp("arbitrary", "arbitrary", "arbitrary"),
        name="s5_chunks",
    )(u3, m_op, b_op, c_op, a_pw, jnp.asarray(rep.astype(np.float32), BF16))


def _s5_out_body(x_ref, y3_ref, g_ref, d_ref, w_ref, b_ref, o_ref, y_sc):
    tm, d = x_ref.shape
    rows = tm // S5_L
    for j in range(d // LANES):
        for tau in range(S5_L):
            y_sc[j, pl.ds(tau, rows, stride=S5_L), :] = y3_ref[j, :, tau * LANES:(tau + 1) * LANES]
    x = x_ref[...]
    u = _rms(x, g_ref[...])
    y_ssm = jnp.concatenate([y_sc[j] for j in range(d // LANES)], axis=1)
    y = jax.nn.gelu(y_ssm + d_ref[...] * u).astype(BF16)
    z = _dot(y, w_ref[...]) + b_ref[...]
    o_ref[...] = x + z[:, :d] * jax.nn.sigmoid(z[:, d:])


def _s5_out(x2, y3, g, d_skip, w_glu, b_glu, tm=512):
    t, d = x2.shape
    n_j = d // LANES
    return pl.pallas_call(
        _s5_out_body,
        grid=(t // tm,),
        in_specs=[
            pl.BlockSpec((tm, d), lambda i: (i, 0)),
            pl.BlockSpec((n_j, tm // S5_L, S5_L * LANES), lambda i: (0, i, 0)),
            _const_spec((1, d)), _const_spec((1, d)),
            _const_spec((d, 2 * d)), _const_spec((1, 2 * d)),
        ],
        out_specs=pl.BlockSpec((tm, d), lambda i: (i, 0)),
        out_shape=jax.ShapeDtypeStruct((t, d), F32),
        scratch_shapes=[pltpu.VMEM((n_j, tm, LANES), F32)],
        compiler_params=_cp("parallel"),
        name="s5_out",
    )(x2, y3, _row(g), _row(d_skip), w_glu.astype(BF16), _row(b_glu))


def _s5_layer(x3, g, lam_re, lam_im, log_step, b_re, b_im, c_re, c_im, d_skip, w_glu, b_glu):
    b, s, d = x3.shape
    x2 = x3.reshape(b * s, d)
    m_op, b_op, c_op, a_pw = _s5_operators(lam_re, lam_im, log_step, b_re, b_im, c_re, c_im)
    u3 = _s5_regroup(x2, g)
    y3 = _s5_chunks(u3, m_op, b_op, c_op, a_pw, b)
    return _s5_out(x2, y3, g, d_skip, w_glu, b_glu).reshape(b, s, d)


NSA_HEAD_DIM = 64
NSA_HEADS = 16
NSA_KV_GROUPS = 4
NSA_REP = NSA_HEADS // NSA_KV_GROUPS
NSA_CMP_BLOCK = 32
NSA_CMP_STRIDE = 16
NSA_SLC_BLOCK = 64
NSA_TOP_N = 16
NSA_LOCAL_BLOCKS = 2
NSA_WINDOW = 512
ROPE_THETA = 500000.0
ROPE_DIMS = NSA_HEAD_DIM // 4
NEG_INF = -1e30
BIG = 1e9
LOG2_E = 1.4426950408889634
NSA_MAX_SEL_BLOCKS = 64
NSA_NORM_SECTIONS = 8
NSA_SEL_TILE = 256


def _rope_body(pos_ref, inv_ref, c_ref, s1_ref, s2_ref):
    ang = pos_ref[...].astype(F32) * inv_ref[...]
    lane = lax.broadcasted_iota(jnp.int32, ang.shape, 1) % NSA_HEAD_DIM
    cos, sin = jnp.cos(ang), jnp.sin(ang)
    half = ROPE_DIMS // 2
    c_ref[...] = jnp.where(lane < ROPE_DIMS, cos, 1.0)
    s1_ref[...] = jnp.where(lane < half, -sin, 0.0)
    s2_ref[...] = jnp.where(lane < half, 0.0, jnp.where(lane < ROPE_DIMS, sin, 0.0))


def _rope_tables(positions):
    s = positions.shape[0]
    half = ROPE_DIMS // 2
    inv_freq = ROPE_THETA ** (-jnp.arange(half, dtype=F32) / half)
    lane = np.arange(LANES) % NSA_HEAD_DIM
    inv_lane = jnp.where(jnp.asarray(lane < ROPE_DIMS), inv_freq[lane % half], 0.0).reshape(1, LANES)
    ts = min(s, 512)
    spec = pl.BlockSpec((ts, LANES), lambda i: (i, 0))
    return pl.pallas_call(
        _rope_body,
        grid=(s // ts,),
        in_specs=[pl.BlockSpec((ts, 1), lambda i: (i, 0)), _const_spec((1, LANES))],
        out_specs=[spec, spec, spec],
        out_shape=[jax.ShapeDtypeStruct((s, LANES), F32)] * 3,
        compiler_params=_cp("parallel"),
        name="rope_tables",
    )(positions.reshape(s, 1), inv_lane)


def _nsa_proj_body(x_ref, g_ref, wn_ref, wv_ref, wg_ref, gain_ref, bd_ref, c_ref, s1_ref, s2_ref,
                   q_ref, kc_ref, kw_ref, ksa_ref, vc_ref, vs_ref, vw_ref, gt_ref, row_sc):
    tm, d = x_ref.shape
    sw = 2 * LANES
    s_idx = pl.program_id(1)

    def store_token_groups(ref, z):
        st = NSA_CMP_STRIDE
        for lt in range(sw // LANES):
            row_sc[lt] = z[:, lt * LANES:(lt + 1) * LANES]
        for j in range(st):
            for lt in range(sw // LANES):
                ref[:, j * sw + lt * LANES:j * sw + (lt + 1) * LANES] = (
                    row_sc[lt, pl.ds(j, tm // st, stride=st), :].astype(BF16))

    h = _rms(x_ref[...], g_ref[...]).astype(BF16)
    c2 = jnp.concatenate([c_ref[...]] * 2, axis=1)
    s1 = jnp.concatenate([s1_ref[...]] * 2, axis=1)
    s2 = jnp.concatenate([s2_ref[...]] * 2, axis=1)
    blk = (s_idx * tm + lax.broadcasted_iota(jnp.int32, (tm, 1), 0)) // NSA_SLC_BLOCK
    lane = lax.broadcasted_iota(jnp.int32, (tm, sw), 1) % LANES
    blk_id = jnp.where(lane - NSA_HEAD_DIM == blk, NEG_INF, 0.0)
    inv_dh = 1.0 / NSA_HEAD_DIM
    for sec in range(NSA_NORM_SECTIONS):
        z = _dot(h, wn_ref[:, sec * sw:(sec + 1) * sw])
        aug = sec >= 6
        ss = _dot((z * z).astype(BF16), bd_ref[1 if aug else 0]) * inv_dh
        z = z * lax.rsqrt(ss + RMS_EPS) * gain_ref[:, sec * sw:(sec + 1) * sw]
        z = z * c2 + pltpu.roll(z, sw - ROPE_DIMS // 2, axis=1) * s1 + pltpu.roll(z, ROPE_DIMS // 2, axis=1) * s2
        if sec < 4:
            q_ref[:, sec * sw:(sec + 1) * sw] = z.astype(BF16)
        elif sec == 4:
            store_token_groups(kc_ref, z)
        elif sec == 5:
            for g in range(NSA_KV_GROUPS):
                kw_ref[g] = z[:, g * NSA_HEAD_DIM:(g + 1) * NSA_HEAD_DIM].astype(BF16)
        else:
            ksa_ref[:, (sec - 6) * sw:(sec - 5) * sw] = (z + blk_id).astype(BF16)
    v = _dot(h, wv_ref[...])
    store_token_groups(vc_ref, v[:, 0:sw])
    ones_col = jnp.where(lax.broadcasted_iota(jnp.int32, (tm, NSA_HEAD_DIM), 1) == 0, 1.0, 0.0)
    for g in range(NSA_KV_GROUPS):
        vs_g = v[:, sw + g * NSA_HEAD_DIM:sw + (g + 1) * NSA_HEAD_DIM]
        vw_g = v[:, 2 * sw + g * NSA_HEAD_DIM:2 * sw + (g + 1) * NSA_HEAD_DIM]
        vs_aug = jnp.concatenate([vs_g, ones_col], axis=1)
        for c in range(tm // NSA_SEL_TILE):
            vs_ref[g, c] = vs_aug[c * NSA_SEL_TILE:(c + 1) * NSA_SEL_TILE].T.astype(BF16)
        vw_ref[g] = jnp.concatenate([vw_g, ones_col], axis=1).astype(BF16)
    gt_ref[...] = jax.nn.sigmoid(_dot(h, wg_ref[...]))


def _nsa_proj(x3, g_mix, w_in, q_gain, k_gain, rope, tm=512):
    b, s, d = x3.shape
    hh, gg, dh = NSA_HEADS, NSA_KV_GROUPS, NSA_HEAD_DIM
    qd, gw = hh * dh, gg * dh
    tm = min(tm, s)
    wq = w_in[:, :qd]
    wkv = w_in[:, qd:qd + 6 * gw].reshape(d, 3, 2, gw)
    wgate = jnp.pad(w_in[:, qd + 6 * gw:], ((0, 0), (0, LANES - 3 * hh)))

    def spread(a):
        a = a.reshape(a.shape[:-1] + (gg, dh))
        return jnp.pad(a, [(0, 0)] * (a.ndim - 1) + [(0, LANES - dh)]).reshape(a.shape[:-2] + (gg * LANES,))

    w_norm = jnp.concatenate([wq, wkv[:, 0, 0], wkv[:, 2, 0], spread(wkv[:, 1, 0])], axis=1).astype(BF16)
    w_v = jnp.concatenate([wkv[:, 0, 1], wkv[:, 1, 1], wkv[:, 2, 1]], axis=1).astype(BF16)
    gain = jnp.concatenate([jnp.tile(q_gain, hh) * (dh ** -0.5 * LOG2_E), jnp.tile(k_gain[0], gg), jnp.tile(k_gain[2], gg),
                            spread(jnp.tile(k_gain[1], gg))]).reshape(1, -1).astype(F32)
    lane = np.arange(2 * LANES)
    bd = np.stack([(lane[:, None] // dh == lane[None, :] // dh), (lane[:, None] // LANES == lane[None, :] // LANES)])
    bd = jnp.asarray(bd.astype(np.float32), BF16)
    n_norm = NSA_NORM_SECTIONS * 2 * LANES
    rope_spec = pl.BlockSpec((tm, LANES), lambda i, j: (j, 0))
    tok = lambda w: pl.BlockSpec((None, tm, w), lambda i, j: (i, j, 0))
    grp = pl.BlockSpec((None, gg, tm, dh), lambda i, j: (i, 0, j, 0))
    grp_v = pl.BlockSpec((None, gg, tm, LANES), lambda i, j: (i, 0, j, 0))
    assert tm % NSA_SEL_TILE == 0
    grp_vt = pl.BlockSpec((None, gg, tm // NSA_SEL_TILE, LANES, NSA_SEL_TILE), lambda i, j: (i, 0, j, 0, 0))
    st = NSA_CMP_STRIDE
    cmp_in = pl.BlockSpec((None, tm // st, st * gw), lambda i, j: (i, j, 0))
    sds = jax.ShapeDtypeStruct
    return pl.pallas_call(
        _nsa_proj_body,
        grid=(b, s // tm),
        in_specs=[tok(d), _const_spec((1, d)), _const_spec((d, n_norm)), _const_spec((d, 3 * gw)),
                  _const_spec((d, LANES)), _const_spec((1, n_norm)), _const_spec((2, 2 * LANES, 2 * LANES)),
                  rope_spec, rope_spec, rope_spec],
        out_specs=[tok(qd), cmp_in, grp, tok(gg * LANES), cmp_in, grp_vt, grp_v, tok(LANES)],
        out_shape=[sds((b, s, qd), BF16), sds((b, s // st, st * gw), BF16), sds((b, gg, s, dh), BF16),
                   sds((b, s, gg * LANES), BF16), sds((b, s // st, st * gw), BF16),
                   sds((b, gg, s // NSA_SEL_TILE, LANES, NSA_SEL_TILE), BF16),
                   sds((b, gg, s, LANES), BF16), sds((b, s, LANES), F32)],
        scratch_shapes=[pltpu.VMEM((gw // LANES, tm, LANES), F32)],
        compiler_params=_cp("parallel", "parallel"),
        name="nsa_proj",
    )(x3, _row(g_mix), w_norm, w_v, wgate.astype(BF16), gain, bd, *rope)


def _nsa_compress_body(kin_ref, vin_ref, wa_ref, wb_ref, pa_ref, pb_ref, b1_ref, w2_ref, k_ref, vt_ref, q_sc, *, n_cmp):
    nr = kin_ref.shape[0]
    dh = NSA_HEAD_DIM
    row = lax.broadcasted_iota(jnp.int32, (nr, 1), 0)
    for c, in_ref in enumerate((kin_ref, vin_ref)):
        x = in_ref[...].astype(F32)
        first = _dot((x + pa_ref[c]).astype(BF16), wa_ref[c])
        q_sc[0:nr, :] = _dot((x + pb_ref[c]).astype(BF16), wb_ref[c])
        q_sc[nr:nr + SUBLANES, :] = jnp.zeros((SUBLANES, q_sc.shape[1]), F32)
        hid = jax.nn.gelu(first + q_sc[1:nr + 1, :] + b1_ref[c])
        comp = jnp.where(row < n_cmp, _dot(hid.astype(BF16), w2_ref[c]), 0.0)
        if c == 0:
            for g in range(NSA_KV_GROUPS):
                k_ref[g] = comp[:, g * dh:(g + 1) * dh].astype(BF16)
        else:
            comp_t = comp.T
            for g in range(NSA_KV_GROUPS):
                vt_ref[g] = comp_t[g * dh:(g + 1) * dh, :].astype(BF16)


def _nsa_compress(kc16, vc16, cmp_pos, cmp_w1, cmp_b1, cmp_w2):
    b, nr, _ = kc16.shape
    gg, dh, st = NSA_KV_GROUPS, NSA_HEAD_DIM, NSA_CMP_STRIDE
    gw = gg * dh
    s = nr * st
    n_cmp = (s - NSA_CMP_BLOCK) // st + 1
    eye = jnp.eye(gg, dtype=F32)
    w1 = cmp_w1.reshape(2, 2, st, dh, dh)
    wexp = jnp.einsum('chjde,gf->chjgdfe', w1, eye).reshape(2, 2, st * gw, gw).astype(BF16)
    pos = jnp.broadcast_to(cmp_pos.reshape(2, 2, st, 1, dh), (2, 2, st, gg, dh)).reshape(2, 2, 1, st * gw).astype(F32)
    b1 = jnp.tile(cmp_b1, (1, gg)).reshape(2, 1, gw).astype(F32)
    w2 = jnp.einsum('cde,gf->cgdfe', cmp_w2, eye).reshape(2, gw, gw).astype(BF16)
    full = lambda shape: _const_spec(shape)
    return pl.pallas_call(
        functools.partial(_nsa_compress_body, n_cmp=n_cmp),
        grid=(b,),
        in_specs=[pl.BlockSpec((None, nr, st * gw), lambda i: (i, 0, 0)),
                  pl.BlockSpec((None, nr, st * gw), lambda i: (i, 0, 0)),
                  full((2, st * gw, gw)), full((2, st * gw, gw)), full((2, 1, st * gw)), full((2, 1, st * gw)),
                  full((2, 1, gw)), full((2, gw, gw))],
        out_specs=[pl.BlockSpec((None, gg, nr, dh), lambda i: (i, 0, 0, 0)),
                   pl.BlockSpec((None, gg, dh, nr), lambda i: (i, 0, 0, 0))],
        out_shape=[jax.ShapeDtypeStruct((b, gg, nr, dh), BF16), jax.ShapeDtypeStruct((b, gg, dh, nr), BF16)],
        scratch_shapes=[pltpu.VMEM((nr + SUBLANES, gw), F32)],
        compiler_params=_cp("parallel"),
        name="nsa_compress",
    )(kc16, vc16, wexp[:, 0], wexp[:, 1], pos[:, 0], pos[:, 1], b1, w2)


def _stacked_row_index(n_rep, tq):
    assert tq & (tq - 1) == 0
    return jnp.bitwise_and(lax.broadcasted_iota(jnp.int32, (n_rep * tq, 1), 0), tq - 1)


def _nt_dot(a, b):
    return lax.dot_general(a, b, (((1,), (1,)), ((), ())), preferred_element_type=F32)


def _nsa_cmp_select_body(q_ref, kc_ref, vt_ref, ovl_ref, oc_ref, ns_ref, sc_sc, *, n_levels):
    tq = q_ref.shape[0]
    nr = kc_ref.shape[0]
    dh = NSA_HEAD_DIM
    nb = NSA_MAX_SEL_BLOCKS
    q0 = pl.program_id(2) * tq
    t_lane = q0 + lax.broadcasted_iota(jnp.int32, (1, tq), 1)
    sub = lax.broadcasted_iota(jnp.int32, (SUBLANES, 1), 0)
    span = SUBLANES * NSA_SLC_BLOCK
    level = (q0 + tq - 1) // span

    def at_level(lv):
        n_tiles = lv + 1
        nbl = n_tiles * SUBLANES
        nrl = min(nr, -(-(n_tiles * span // NSA_CMP_STRIDE) // LANES) * LANES)
        kc, vt = kc_ref[0:nrl, :], vt_ref[:, 0:nrl]
        end_row = lax.broadcasted_iota(jnp.int32, (nrl, 1), 0) * NSA_CMP_STRIDE + (NSA_CMP_BLOCK - 1)
        mask_t = end_row <= t_lane
        psum_t = jnp.zeros((nrl, tq), F32)
        outs_t = []
        for h in range(NSA_REP):
            qh = q_ref[:, h * dh:(h + 1) * dh]
            s_t = jnp.where(mask_t, _nt_dot(kc, qh), NEG_INF)
            m_t = jnp.maximum(jnp.max(s_t, axis=0, keepdims=True), 0.1 * NEG_INF)
            e_t = jnp.exp2(s_t - m_t)
            den_t = jnp.sum(e_t, axis=0, keepdims=True)
            p_t = e_t * (1.0 / jnp.where(den_t > 0.0, den_t, 1.0))
            psum_t = psum_t + p_t
            outs_t.append(_dot(vt, p_t.astype(BF16)))
        oc_ref[...] = jnp.concatenate(outs_t, axis=0).T.astype(BF16)

        p_hi = psum_t.astype(BF16)
        p_lo = (psum_t - p_hi.astype(F32)).astype(BF16)
        ovl = ovl_ref[0:-(-nbl // 16) * 16, 0:nrl]
        imp = (_dot(ovl, p_hi) + _dot(ovl, p_lo))[0:nbl]
        blk = lax.broadcasted_iota(jnp.int32, (nbl, 1), 0)
        cur = t_lane // NSA_SLC_BLOCK
        valid = blk <= cur
        forced = (blk == 0) | (blk >= cur - (NSA_LOCAL_BLOCKS - 1))
        score = jnp.where(valid, jnp.where(forced, BIG, imp), -BIG)
        sc_sc[0:nbl, :] = score
        tiles = [score[a * SUBLANES:(a + 1) * SUBLANES] for a in range(n_tiles)]
        ranks = [jnp.zeros((SUBLANES, tq), F32) for _ in range(n_tiles)]
        for j in range(nbl):
            sj = sc_sc[j:j + 1, :]
            jt = j // SUBLANES
            for a in range(n_tiles):
                if a > jt:
                    ahead = jnp.where(sj >= tiles[a], 1.0, 0.0)
                elif a < jt:
                    ahead = jnp.where(sj > tiles[a], 1.0, 0.0)
                else:
                    ahead = jnp.where(sub > j - a * SUBLANES,
                                      jnp.where(sj >= tiles[a], 1.0, 0.0), jnp.where(sj > tiles[a], 1.0, 0.0))
                ranks[a] = ranks[a] + ahead
        notsel = [jnp.where(r < float(NSA_TOP_N), 0.0, 1.0) for r in ranks]
        notsel = jnp.concatenate(notsel + [jnp.ones((LANES - nbl, tq), F32)], axis=0)
        ns_ref[...] = notsel.T[:, :nb].astype(BF16)

    for lv in range(n_levels):
        pl.when(level == lv)(functools.partial(at_level, lv))


def _nsa_cmp_select(q, k_cmp, vt_cmp, tq=256):
    b, s, qd = q.shape
    gg, dh = NSA_KV_GROUPS, NSA_HEAD_DIM
    nr = k_cmp.shape[2]
    nb = NSA_MAX_SEL_BLOCKS
    tq = min(tq, s)
    assert s // NSA_SLC_BLOCK <= nb
    c_start = np.arange(nr) * NSA_CMP_STRIDE
    c_end = c_start + NSA_CMP_BLOCK - 1
    s_start = np.arange(nb) * NSA_SLC_BLOCK
    s_end = s_start + NSA_SLC_BLOCK - 1
    ovl = ((c_start[None, :] <= s_end[:, None]) & (c_end[None, :] >= s_start[:, None])).astype(np.float32)
    n_levels = -(-s // (SUBLANES * NSA_SLC_BLOCK))
    return pl.pallas_call(
        functools.partial(_nsa_cmp_select_body, n_levels=n_levels),
        grid=(b, gg, s // tq),
        in_specs=[pl.BlockSpec((None, tq, NSA_REP * dh), lambda i, g, j: (i, j, g)),
                  pl.BlockSpec((None, None, nr, dh), lambda i, g, j: (i, g, 0, 0)),
                  pl.BlockSpec((None, None, dh, nr), lambda i, g, j: (i, g, 0, 0)),
                  _const_spec((nb, nr))],
        out_specs=[pl.BlockSpec((None, tq, NSA_REP * dh), lambda i, g, j: (i, j, g)),
                   pl.BlockSpec((None, None, tq, nb), lambda i, g, j: (i, g, j, 0))],
        out_shape=[jax.ShapeDtypeStruct((b, s, qd), BF16), jax.ShapeDtypeStruct((b, gg, s, nb), BF16)],
        scratch_shapes=[pltpu.VMEM((nb, tq), F32)],
        compiler_params=_cp("parallel", "parallel", "parallel"),
        name="nsa_cmp_select",
    )(q, k_cmp, vt_cmp, jnp.asarray(ovl, BF16))


def _nsa_sel_body(q_ref, ns_ref, k_ref, vt_ref, o_ref, qa_sc, s0_sc, s1_sc, m_sc, acc_sc):
    tq = q_ref.shape[0]
    dh = NSA_HEAD_DIM
    qi = pl.program_id(2)
    for h in range(NSA_REP):
        qa_sc[h * tq:(h + 1) * tq, :] = jnp.concatenate([q_ref[:, h * dh:(h + 1) * dh], ns_ref[...]], axis=1)
    m_sc[...] = jnp.full(m_sc.shape, NEG_INF, F32)
    acc_sc[...] = jnp.zeros(acc_sc.shape, F32)
    k_pos = lax.broadcasted_iota(jnp.int32, (tq, 1), 0)
    t_pos = lax.broadcasted_iota(jnp.int32, (1, tq), 1)

    def scores(kt, s_sc):
        start = pl.multiple_of(kt * tq, tq)
        s_sc[...] = _nt_dot(k_ref[pl.ds(start, tq), :], qa_sc[...])

    def attend(kt, s_sc, diagonal):
        vt = vt_ref[kt]
        for h in range(NSA_REP):
            cols = slice(h * tq, (h + 1) * tq)
            sh = s_sc[:, cols]
            if diagonal:
                sh = jnp.where(k_pos <= t_pos, sh, NEG_INF)
            m_old = m_sc[:, cols]
            m_new = jnp.maximum(m_old, jnp.max(sh, axis=0, keepdims=True))
            p = jnp.exp2(sh - m_new)
            acc_sc[:, cols] = jnp.exp2(m_old - m_new) * acc_sc[:, cols] + _dot(vt, p.astype(BF16))
            m_sc[:, cols] = m_new

    def tile_pair(i, carry):
        scores(2 * i + 1, s1_sc)
        attend(2 * i, s0_sc, False)
        scores(2 * i + 2, s0_sc)
        attend(2 * i + 1, s1_sc, False)
        return carry

    scores(0, s0_sc)
    lax.fori_loop(0, qi // 2, tile_pair, 0)

    @pl.when(qi % 2 == 0)
    def _():
        attend(qi, s0_sc, True)

    @pl.when(qi % 2 == 1)
    def _():
        scores(qi, s1_sc)
        attend(qi - 1, s0_sc, False)
        attend(qi, s1_sc, True)

    outs = []
    for h in range(NSA_REP):
        acc = acc_sc[:, h * tq:(h + 1) * tq].T
        outs.append(acc[:, :dh] / acc[:, dh:dh + 1])
    o_ref[...] = jnp.concatenate(outs, axis=1).astype(BF16)


def _nsa_selected(q, notsel, ksa, vst):
    b, s, qd = q.shape
    gg, dh, rr = NSA_KV_GROUPS, NSA_HEAD_DIM, NSA_REP
    tq = NSA_SEL_TILE
    assert s % tq == 0 and tq % LANES == 0
    return pl.pallas_call(
        _nsa_sel_body,
        grid=(b, gg, s // tq),
        in_specs=[pl.BlockSpec((None, tq, rr * dh), lambda i, g, j: (i, j, g)),
                  pl.BlockSpec((None, None, tq, NSA_MAX_SEL_BLOCKS), lambda i, g, j: (i, g, j, 0)),
                  pl.BlockSpec((None, s, LANES), lambda i, g, j: (i, 0, g)),
                  pl.BlockSpec((None, None, s // tq, LANES, tq), lambda i, g, j: (i, g, 0, 0, 0))],
        out_specs=pl.BlockSpec((None, tq, rr * dh), lambda i, g, j: (i, j, g)),
        out_shape=jax.ShapeDtypeStruct((b, s, qd), BF16),
        scratch_shapes=[pltpu.VMEM((rr * tq, LANES), BF16), pltpu.VMEM((tq, rr * tq), F32),
                        pltpu.VMEM((tq, rr * tq), F32), pltpu.VMEM((1, rr * tq), F32),
                        pltpu.VMEM((LANES, rr * tq), F32)],
        compiler_params=_cp("parallel", "parallel", "arbitrary"),
        name="nsa_selected",
    )(q, notsel, ksa, vst)


def _nsa_window_body(q_ref, *refs, n_kv):
    k_refs, v_refs, o_ref = refs[:n_kv], refs[n_kv:2 * n_kv], refs[2 * n_kv]
    tq = q_ref.shape[0]
    dh = NSA_HEAD_DIM
    qi = pl.program_id(2)
    qs = jnp.concatenate([q_ref[:, h * dh:(h + 1) * dh] for h in range(NSA_REP)], axis=0)
    t_pos = lax.broadcasted_iota(jnp.int32, (tq, 1), 0)
    k_rel = lax.broadcasted_iota(jnp.int32, (1, tq), 1)
    scores = [_nt_dot(qs, k_refs[j][...]) for j in range(n_kv)]
    outs = []
    for h in range(NSA_REP):
        parts = []
        for j in range(n_kv):
            sh = scores[j][h * tq:(h + 1) * tq]
            back = (n_kv - 1 - j) * tq
            in_seq = qi >= n_kv - 1 - j
            if j == n_kv - 1:
                parts.append(jnp.where(k_rel <= t_pos, sh, NEG_INF))
            elif j == 0:
                limit = jnp.where(in_seq, NSA_WINDOW, -(2 ** 30))
                parts.append(jnp.where(t_pos - k_rel + back < limit, sh, NEG_INF))
            else:
                parts.append(sh + jnp.where(in_seq, 0.0, NEG_INF))
        m = functools.reduce(jnp.maximum, [jnp.max(p_, axis=-1, keepdims=True) for p_ in parts])
        parts = [jnp.exp2(p_ - m) for p_ in parts]
        o = functools.reduce(jnp.add, [_dot(parts[j].astype(BF16), v_refs[j][...]) for j in range(n_kv)])
        outs.append(o[:, :dh] / o[:, dh:dh + 1])
    o_ref[...] = jnp.concatenate(outs, axis=1).astype(BF16)


def _nsa_window(q, kw, vw, tq=256):
    b, s, qd = q.shape
    gg, dh, rr = NSA_KV_GROUPS, NSA_HEAD_DIM, NSA_REP
    tq = min(tq, s)
    assert NSA_WINDOW % tq == 0
    n_kv = NSA_WINDOW // tq + 1
    kv_spec = lambda off, w: pl.BlockSpec((None, None, tq, w), lambda i, g, j: (i, g, jnp.maximum(j - off, 0), 0))
    k_specs = [kv_spec(n_kv - 1 - j, dh) for j in range(n_kv)]
    v_specs = [kv_spec(n_kv - 1 - j, LANES) for j in range(n_kv)]
    return pl.pallas_call(
        functools.partial(_nsa_window_body, n_kv=n_kv),
        grid=(b, gg, s // tq),
        in_specs=[pl.BlockSpec((None, tq, rr * dh), lambda i, g, j: (i, j, g))] + k_specs + v_specs,
        out_specs=pl.BlockSpec((None, tq, rr * dh), lambda i, g, j: (i, j, g)),
        out_shape=jax.ShapeDtypeStruct((b, s, qd), BF16),
        compiler_params=_cp("parallel", "parallel", "parallel"),
        name="nsa_window",
    )(q, *([kw] * n_kv), *([vw] * n_kv))


def _nsa_out_body(x_ref, oc_ref, os_ref, ow_ref, gt_ref, ex_ref, w_ref, o_ref):
    qd = oc_ref.shape[1]
    gt = gt_ref[...]
    g_hi = gt.astype(BF16)
    g_lo = (gt - g_hi.astype(F32)).astype(BF16)
    o = jnp.zeros(oc_ref.shape, F32)
    for br, ref in enumerate((oc_ref, os_ref, ow_ref)):
        ex = ex_ref[:, br * qd:(br + 1) * qd]
        o = o + (_dot(g_hi, ex) + _dot(g_lo, ex)) * ref[...].astype(F32)
    o_ref[...] = x_ref[...] + _dot(o.astype(BF16), w_ref[...])


def _nsa_out(x2, oc, osel, ow, gates, w_out, tm=512):
    t, d = x2.shape
    qd = oc.shape[1]
    tm = min(tm, t)
    col = np.arange(3 * qd)
    expand = (np.arange(LANES)[:, None] == (col // qd) * NSA_HEADS + (col % qd) // NSA_HEAD_DIM).astype(np.float32)
    tok = lambda w: pl.BlockSpec((tm, w), lambda i: (i, 0))
    return pl.pallas_call(
        _nsa_out_body,
        grid=(t // tm,),
        in_specs=[tok(d), tok(qd), tok(qd), tok(qd), tok(LANES), _const_spec((LANES, 3 * qd)), _const_spec((qd, d))],
        out_specs=tok(d),
        out_shape=jax.ShapeDtypeStruct((t, d), F32),
        compiler_params=_cp("parallel"),
        name="nsa_out",
    )(x2, oc, osel, ow, gates, jnp.asarray(expand, BF16), w_out.astype(BF16))


def _nsa_layer(x3, positions, g_mix, w_in, q_gain, k_gain, cmp_pos, cmp_w1, cmp_b1, cmp_w2, w_out):
    b, s, d = x3.shape
    rope = _rope_tables(positions)
    q, kc, kw, ksa, vc, vs, vw, gates = _nsa_proj(x3, g_mix, w_in, q_gain, k_gain, rope)
    k_cmp, vt_cmp = _nsa_compress(kc, vc, cmp_pos, cmp_w1, cmp_b1, cmp_w2)
    oc, notsel = _nsa_cmp_select(q, k_cmp, vt_cmp)
    osel = _nsa_selected(q, notsel, ksa, vs)
    ow = _nsa_window(q, kw, vw)
    t = b * s
    flat = lambda a: a.reshape(t, a.shape[-1])
    return _nsa_out(x3.reshape(t, d), flat(oc), flat(osel), flat(ow), flat(gates), w_out).reshape(b, s, d)


def kernel(x, positions, norm_mix, norm_ffn, ffn_w_in, ffn_w_out, conv_w_in, conv_b_in, conv_w_dw, conv_b_dw, conv_ln_g, conv_ln_b, conv_w_out, nsa_w_in, nsa_q_gain, nsa_k_gain, nsa_cmp_pos, nsa_cmp_w1, nsa_cmp_b1, nsa_cmp_w2, nsa_w_out, s5_lam_re, s5_lam_im, s5_log_step, s5_b_re, s5_b_im, s5_c_re, s5_c_im, s5_d, s5_w_glu, s5_b_glu, pool_w, pool_scale):
    b, s, d = x.shape
    depth = norm_mix.shape[0]
    for i in range(depth):
        m, j = i % N_MIXERS, i // N_MIXERS
        if m == 0:
            x = _conv_layer(x, norm_mix[i], conv_w_in[j], conv_b_in[j], conv_w_dw[j], conv_b_dw[j],
                            conv_ln_g[j], conv_ln_b[j], conv_w_out[j])
        elif m == 1:
            x = _nsa_layer(x, positions, norm_mix[i], nsa_w_in[j], nsa_q_gain[j], nsa_k_gain[j], nsa_cmp_pos[j],
                           nsa_cmp_w1[j], nsa_cmp_b1[j], nsa_cmp_w2[j], nsa_w_out[j])
        elif m == 2:
            x = _s5_layer(x, norm_mix[i], s5_lam_re[j], s5_lam_im[j], s5_log_step[j], s5_b_re[j], s5_b_im[j],
                          s5_c_re[j], s5_c_im[j], s5_d[j], s5_w_glu[j], s5_b_glu[j])
        elif m == 3:
            x = _pool_layer(x, norm_mix[i], pool_w[j], pool_scale[j])
        x = _ffn(x.reshape(b * s, d), norm_ffn[i], ffn_w_in[i], ffn_w_out[i]).reshape(b, s, d)
    return x
```

```python
import functools
import math

import numpy as np
import jax
import jax.numpy as jnp
from jax import lax
from jax.experimental import pallas as pl
from jax.experimental.pallas import tpu as pltpu

F32 = jnp.float32
BF16 = jnp.bfloat16

RMS_EPS = 1e-6
LN_EPS = 1e-5
N_MIXERS = 4
CONV_WIDTH = 31
POOL_WINDOWS = (2, 4, 8, 16)

V7X_VMEM_LIMIT_BYTES = 56 * 1024 * 1024
SUBLANES = 8
LANES = 128


def _cp(*sem):
    return pltpu.CompilerParams(dimension_semantics=sem, vmem_limit_bytes=V7X_VMEM_LIMIT_BYTES)


def _const_spec(shape):
    nd = len(shape)
    return pl.BlockSpec(shape, lambda *_: (0,) * nd, pipeline_mode=pl.Buffered(1))


def _dot(a, b):
    return jnp.dot(a, b, preferred_element_type=F32)


def _rms(x, g):
    ms = jnp.mean(x * x, axis=-1, keepdims=True)
    return x * lax.rsqrt(ms + RMS_EPS) * g


def _row(v):
    return v.reshape(1, -1).astype(F32)


FFN_CHUNK = 256


def _ffn_body(x_ref, g_ref, wg_ref, wu_ref, wo_ref, o_ref, act_sc):
    x = x_ref[...]
    h = _rms(x, g_ref[...]).astype(BF16)
    n_chunks = act_sc.shape[1] // FFN_CHUNK
    for c in range(n_chunks):
        sl = slice(c * FFN_CHUNK, (c + 1) * FFN_CHUNK)
        gate = _dot(h, wg_ref[:, sl])
        up = _dot(h, wu_ref[:, sl])
        act_sc[:, sl] = (gate * jax.nn.sigmoid(gate) * up).astype(BF16)
    o_ref[...] = x + _dot(act_sc[...], wo_ref[...])


def _ffn(x2, g, w_in, w_out, tm=512):
    t, d = x2.shape
    f = w_out.shape[0]
    w_in = w_in.astype(BF16)
    w_out = w_out.astype(BF16)
    return pl.pallas_call(
        _ffn_body,
        grid=(t // tm,),
        in_specs=[
            pl.BlockSpec((tm, d), lambda i: (i, 0)),
            _const_spec((1, d)),
            pl.BlockSpec((d, f), lambda i: (0, 0), pipeline_mode=pl.Buffered(1)),
            pl.BlockSpec((d, f), lambda i: (0, 1), pipeline_mode=pl.Buffered(1)),
            _const_spec((f, d)),
        ],
        out_specs=pl.BlockSpec((tm, d), lambda i: (i, 0)),
        out_shape=jax.ShapeDtypeStruct((t, d), F32),
        scratch_shapes=[pltpu.VMEM((tm, f), BF16)],
        compiler_params=_cp("parallel"),
        name="ffn",
    )(x2, _row(g), w_in, w_in, w_out)


CONV_HALO = 32
CONV_ROWS = 64


def _conv_body(x_ref, g_ref, wa_ref, wg_ref, ba_ref, bg_ref, wdw_ref, bdw_ref,
               lng_ref, lnb_ref, wo_ref, o_ref, u_sc, v_sc):
    tm, d = x_ref.shape
    s = pl.program_id(1)

    @pl.when(s == 0)
    def _():
        u_sc[0:CONV_HALO, :] = jnp.zeros((CONV_HALO, d), F32)

    @pl.when(s > 0)
    def _():
        u_sc[0:CONV_HALO, :] = u_sc[tm:tm + CONV_HALO, :]

    x = x_ref[...]
    h = _rms(x, g_ref[...]).astype(BF16)
    a = _dot(h, wa_ref[...]) + ba_ref[...]
    gt = _dot(h, wg_ref[...]) + bg_ref[...]
    u_sc[CONV_HALO:CONV_HALO + tm, :] = a * jax.nn.sigmoid(gt)

    off = CONV_HALO - (CONV_WIDTH - 1)

    def conv_rows(c, carry):
        r0 = pl.multiple_of(c * CONV_ROWS, CONV_ROWS)
        for lt in range(d // LANES):
            lanes = slice(lt * LANES, (lt + 1) * LANES)
            win = u_sc[pl.ds(r0, CONV_ROWS + CONV_HALO), lanes]
            acc = jnp.zeros((CONV_ROWS, LANES), F32) + bdw_ref[:, lanes]
            for sft in range(SUBLANES):
                shifted = pltpu.roll(win, CONV_ROWS + CONV_HALO - sft, axis=0) if sft else win
                for j in range(sft, CONV_HALO + 1, SUBLANES):
                    k = j - off
                    if 0 <= k < CONV_WIDTH:
                        acc = acc + wdw_ref[k:k + 1, lanes] * shifted[j - sft:j - sft + CONV_ROWS]
            v_sc[pl.ds(r0, CONV_ROWS), lanes] = acc
        return carry

    lax.fori_loop(0, tm // CONV_ROWS, conv_rows, 0)

    v = v_sc[...]
    mu = jnp.mean(v, axis=-1, keepdims=True)
    vc = v - mu
    var = jnp.mean(vc * vc, axis=-1, keepdims=True)
    y = vc * lax.rsqrt(var + LN_EPS) * lng_ref[...] + lnb_ref[...]
    y = (y * jax.nn.sigmoid(y)).astype(BF16)
    o_ref[...] = x + _dot(y, wo_ref[...])


def _conv_layer(x3, g, w_in, b_in, w_dw, b_dw, ln_g, ln_b, w_out, tm=512):
    b, s, d = x3.shape
    w_in = w_in.astype(BF16)
    return pl.pallas_call(
        _conv_body,
        grid=(b, s // tm),
        in_specs=[
            pl.BlockSpec((None, tm, d), lambda i, j: (i, j, 0)),
            _const_spec((1, d)),
            pl.BlockSpec((d, d), lambda i, j: (0, 0), pipeline_mode=pl.Buffered(1)),
            pl.BlockSpec((d, d), lambda i, j: (0, 1), pipeline_mode=pl.Buffered(1)),
            pl.BlockSpec((1, d), lambda i, j: (0, 0), pipeline_mode=pl.Buffered(1)),
            pl.BlockSpec((1, d), lambda i, j: (0, 1), pipeline_mode=pl.Buffered(1)),
            _const_spec((CONV_WIDTH, d)),
            _const_spec((1, d)),
            _const_spec((1, d)),
            _const_spec((1, d)),
            _const_spec((d, d)),
        ],
        out_specs=pl.BlockSpec((None, tm, d), lambda i, j: (i, j, 0)),
        out_shape=jax.ShapeDtypeStruct((b, s, d), F32),
        scratch_shapes=[pltpu.VMEM((CONV_HALO + tm, d), F32), pltpu.VMEM((tm, d), F32)],
        compiler_params=_cp("parallel", "arbitrary"),
        name="conv_module",
    )(x3, _row(g), w_in, w_in, _row(b_in), _row(b_in), w_dw.astype(F32), _row(b_dw),
      _row(ln_g), _row(ln_b), w_out.astype(BF16))


POOL_HALO = 16


def _pool_body(x_ref, g_ref, w_ref, sc_ref, o_ref, h_sc):
    tm, d = x_ref.shape
    s = pl.program_id(1)
    grp = d // len(POOL_WINDOWS)

    @pl.when(s == 0)
    def _():
        h_sc[0:POOL_HALO, :] = jnp.zeros((POOL_HALO, d), F32)

    @pl.when(s > 0)
    def _():
        h_sc[0:POOL_HALO, :] = h_sc[tm:tm + POOL_HALO, :]

    x = x_ref[...]
    h_sc[POOL_HALO:POOL_HALO + tm, :] = _rms(x, g_ref[...])
    t = s * tm + lax.broadcasted_iota(jnp.int32, (tm, 1), 0)
    for gi, win in enumerate(POOL_WINDOWS):
        lo = gi * grp
        hcur = h_sc[POOL_HALO:POOL_HALO + tm, lo:lo + grp]
        tot = hcur
        for k in range(1, win):
            tot = tot + h_sc[POOL_HALO - k:POOL_HALO - k + tm, lo:lo + grp]
        cnt = jnp.minimum(t + 1, win).astype(F32)
        pooled = (tot * (1.0 / cnt) - hcur).astype(BF16)
        y = _dot(pooled, w_ref[gi]) * sc_ref[:, lo:lo + grp]
        o_ref[:, lo:lo + grp] = x[:, lo:lo + grp] + y


def _pool_layer(x3, g, w_grp, scale, tm=512):
    b, s, d = x3.shape
    ng, grp, _ = w_grp.shape
    return pl.pallas_call(
        _pool_body,
        grid=(b, s // tm),
        in_specs=[
            pl.BlockSpec((None, tm, d), lambda i, j: (i, j, 0)),
            _const_spec((1, d)),
            _const_spec((ng, grp, grp)),
            _const_spec((1, d)),
        ],
        out_specs=pl.BlockSpec((None, tm, d), lambda i, j: (i, j, 0)),
        out_shape=jax.ShapeDtypeStruct((b, s, d), F32),
        scratch_shapes=[pltpu.VMEM((POOL_HALO + tm, d), F32)],
        compiler_params=_cp("parallel", "arbitrary"),
        name="pool",
    )(x3, _row(g), w_grp.astype(BF16), _row(scale))


S5_L = 8
S5_GPT = 8


def _s5_operators(lam_re, lam_im, log_step, b_re, b_im, c_re, c_im):
    hi = lax.Precision.HIGHEST
    n_g, n_p = lam_re.shape
    n_c = b_re.shape[-1]
    n_j = n_g // S5_GPT
    L = S5_L
    def cmul(a, b):
        return a[0] * b[0] - a[1] * b[1], a[0] * b[1] + a[1] * b[0]

    def cexp(z):
        mag = jnp.exp(z[0])
        return mag * jnp.cos(z[1]), mag * jnp.sin(z[1])

    lr, li = lam_re.astype(F32), lam_im.astype(F32)
    step = jnp.exp(log_step.astype(F32))[:, None]
    lam_bar = cexp((lr * step, li * step))
    inv_den = 1.0 / (lr * lr + li * li)
    zoh = cmul((lam_bar[0] - 1.0, lam_bar[1]), (lr * inv_den, -li * inv_den))
    b_bar = cmul((zoh[0][:, :, None], zoh[1][:, :, None]), (b_re.astype(F32), b_im.astype(F32)))
    c_mat = (c_re.astype(F32), c_im.astype(F32))
    dd = jnp.arange(2 * L, dtype=F32)[:, None, None]
    pw = cexp(((lr * step)[None] * dd, (li * step)[None] * dd))

    pb = cmul((pw[0][:L, :, :, None], pw[1][:L, :, :, None]), (b_bar[0][None], b_bar[1][None]))
    kern = (jnp.einsum('gap,dgpc->gdac', c_mat[0], pb[0], precision=hi)
            - jnp.einsum('gap,dgpc->gdac', c_mat[1], pb[1], precision=hi))
    sig = np.arange(L)[:, None]
    tau = np.arange(L)[None, :]
    lag = np.clip(tau - sig, 0, L - 1)
    kt = kern[:, lag] * jnp.asarray((tau >= sig).astype(np.float32))[None, :, :, None, None]
    kt = kt.reshape(n_j, S5_GPT, L, L, n_c, n_c)
    m_op = kt.transpose(0, 2, 1, 5, 3, 4).reshape(n_j, L * S5_GPT * n_c, L * n_c)

    e = jnp.stack(pb, axis=0)[:, ::-1].transpose(0, 2, 1, 3, 4)
    e = e.reshape(2, n_j, S5_GPT, L, n_p, n_c)
    b_op = e.transpose(1, 3, 2, 5, 0, 4).reshape(n_j, L * S5_GPT * n_c, 2 * n_p)

    f = cmul((c_mat[0][:, None], c_mat[1][:, None]),
             (pw[0][1:L + 1].transpose(1, 0, 2)[:, :, None, :], pw[1][1:L + 1].transpose(1, 0, 2)[:, :, None, :]))
    f = jnp.stack([f[0], -f[1]], axis=0).reshape(2, n_j, S5_GPT, L, n_c, n_p)
    c_op = f.transpose(1, 0, 2, 5, 3, 4).reshape(n_j, 2 * S5_GPT * n_p, L * n_c)

    ak = cexp(((lr * step * L)[None] * dd, (li * step * L)[None] * dd))
    ak = jnp.stack(ak, axis=1).reshape(2 * L, 2, n_j, S5_GPT * n_p)
    a_pw = ak.transpose(2, 0, 1, 3).reshape(n_j, 2 * L, 2 * S5_GPT * n_p)
    return m_op.astype(BF16), b_op.astype(BF16), c_op.astype(BF16), a_pw.astype(F32)


def _s5_regroup_body(x_ref, g_ref, o_ref, u_sc):
    tm, d = x_ref.shape
    u = _rms(x_ref[...], g_ref[...])
    rows = tm // S5_L
    for j in range(d // LANES):
        u_sc[j] = u[:, j * LANES:(j + 1) * LANES]
    for j in range(d // LANES):
        for sg in range(S5_L):
            o_ref[j, :, sg * LANES:(sg + 1) * LANES] = (
                u_sc[j, pl.ds(sg, rows, stride=S5_L), :].astype(BF16))


def _s5_regroup(x2, g, tm=512):
    t, d = x2.shape
    n_j = d // LANES
    return pl.pallas_call(
        _s5_regroup_body,
        grid=(t // tm,),
        in_specs=[pl.BlockSpec((tm, d), lambda i: (i, 0)), _const_spec((1, d))],
        out_specs=pl.BlockSpec((n_j, tm // S5_L, S5_L * LANES), lambda i: (0, i, 0)),
        out_shape=jax.ShapeDtypeStruct((n_j, t // S5_L, S5_L * LANES), BF16),
        scratch_shapes=[pltpu.VMEM((n_j, tm, LANES), F32)],
        compiler_params=_cp("parallel"),
        name="s5_regroup",
    )(x2, _row(g))


def _s5_chunk_body(u_ref, mc_ref, bc_ref, cc_ref, a_ref, rep_ref, y_ref,
                   m_ref, b_ref, c_ref, xr_sc, xi_sc, pr_sc, pi_sc, cr_sc, ci_sc):
    rows, width = u_ref.shape
    half = width // 2
    nt = pl.program_id(2)

    @pl.when((pl.program_id(1) == 0) & (nt == 0))
    def _():
        n_c = width // (S5_L * S5_GPT)
        n_p = half // S5_GPT
        cw = 2 * LANES
        for op_ref, src_ref, rep, row_div, col_div in (
                (m_ref, mc_ref, 0, n_c, n_c), (b_ref, bc_ref, 1, n_c, n_p), (c_ref, cc_ref, 0, n_p, n_c)):
            g_row = (lax.broadcasted_iota(jnp.int32, (width, 1), 0) // row_div) % S5_GPT
            for c0 in range(0, width, cw):
                g_col = ((c0 + lax.broadcasted_iota(jnp.int32, (1, cw), 1)) // col_div) % S5_GPT
                full = _dot(src_ref[...], rep_ref[rep, :, c0:c0 + cw])
                op_ref[:, c0:c0 + cw] = jnp.where(g_row == g_col, full, 0.0).astype(BF16)

    @pl.when(nt == 0)
    def _():
        cr_sc[...] = jnp.zeros_like(cr_sc)
        ci_sc[...] = jnp.zeros_like(ci_sc)

    u = u_ref[...]
    xin = _dot(u, b_ref[...])
    xr, xi = xin[:, :half], xin[:, half:]
    row = lax.broadcasted_iota(jnp.int32, (rows, 1), 0) % SUBLANES
    for shift in (1, 2, 4):
        ar, ai = a_ref[shift:shift + 1, :half], a_ref[shift:shift + 1, half:]
        keep = row >= shift
        sr = jnp.where(keep, pltpu.roll(xr, shift, axis=0), 0.0)
        si = jnp.where(keep, pltpu.roll(xi, shift, axis=0), 0.0)
        xr, xi = xr + ar * sr - ai * si, xi + ar * si + ai * sr
    xr_sc[...] = xr
    xi_sc[...] = xi
    keep = row >= 1
    pr_sc[...] = jnp.where(keep, pltpu.roll(xr, 1, axis=0), 0.0)
    pi_sc[...] = jnp.where(keep, pltpu.roll(xi, 1, axis=0), 0.0)

    akr, aki = a_ref[0:SUBLANES, :half], a_ref[0:SUBLANES, half:]
    a8r, a8i = a_ref[SUBLANES:SUBLANES + 1, :half], a_ref[SUBLANES:SUBLANES + 1, half:]

    def tile_step(t, carry):
        cr, ci = carry
        base = pl.multiple_of(t * SUBLANES, SUBLANES)
        pr_sc[pl.ds(base, SUBLANES), :] = pr_sc[pl.ds(base, SUBLANES), :] + akr * cr - aki * ci
        pi_sc[pl.ds(base, SUBLANES), :] = pi_sc[pl.ds(base, SUBLANES), :] + akr * ci + aki * cr
        lr = xr_sc[pl.ds(base + SUBLANES - 1, 1), :]
        li = xi_sc[pl.ds(base + SUBLANES - 1, 1), :]
        return lr + a8r * cr - a8i * ci, li + a8r * ci + a8i * cr

    cr, ci = lax.fori_loop(0, rows // SUBLANES, tile_step, (cr_sc[...], ci_sc[...]))
    cr_sc[...] = cr
    ci_sc[...] = ci

    y = _dot(u, m_ref[...])
    y = y + _dot(pr_sc[...].astype(BF16), c_ref[0:half, :])
    y = y + _dot(pi_sc[...].astype(BF16), c_ref[half:, :])
    y_ref[...] = y


def _s5_chunks(u3, m_op, b_op, c_op, a_pw, bsz):
    n_j, n_rows, width = u3.shape
    rows_per_seq = n_rows // bsz
    rows = min(512, rows_per_seq)
    nt = rows_per_seq // rows
    half = width // 2
    narrow = m_op.shape[-1]
    col = np.arange(width)
    n_c, n_p = width // (S5_L * S5_GPT), half // S5_GPT
    src_ta = (col // (S5_GPT * n_c)) * n_c + col % n_c
    src_rp = (col // (S5_GPT * n_p)) * n_p + col % n_p
    rep = np.stack([np.arange(narrow)[:, None] == src_ta[None, :], np.arange(narrow)[:, None] == src_rp[None, :]])
    wspec = lambda shape: pl.BlockSpec((None,) + shape, lambda j, b, t: (j, 0, 0), pipeline_mode=pl.Buffered(1))
    return pl.pallas_call(
        _s5_chunk_body,
        grid=(n_j, bsz, nt),
        in_specs=[
            pl.BlockSpec((None, rows, width), lambda j, b, t: (j, b * nt + t, 0)),
            wspec((width, narrow)), wspec((width, narrow)), wspec((width, narrow)),
            wspec((2 * S5_L, width)), _const_spec((2, narrow, width)),
        ],
        out_specs=pl.BlockSpec((None, rows, width), lambda j, b, t: (j, b * nt + t, 0)),
        out_shape=jax.ShapeDtypeStruct((n_j, n_rows, width), F32),
        scratch_shapes=([pltpu.VMEM((width, width), BF16)] * 3 + [pltpu.VMEM((rows, half), F32)] * 4
                        + [pltpu.VMEM((1, half), F32)] * 2),
        compiler_params=_cp("arbitrary", "arbitrary", "arbitrary"),
        name="s5_chunks",
    )(u3, m_op, b_op, c_op, a_pw, jnp.asarray(rep.astype(np.float32), BF16))


def _s5_out_body(x_ref, y3_ref, g_ref, d_ref, w_ref, b_ref, o_ref, y_sc):
    tm, d = x_ref.shape
    rows = tm // S5_L
    for j in range(d // LANES):
        for tau in range(S5_L):
            y_sc[j, pl.ds(tau, rows, stride=S5_L), :] = y3_ref[j, :, tau * LANES:(tau + 1) * LANES]
    x = x_ref[...]
    u = _rms(x, g_ref[...])
    y_ssm = jnp.concatenate([y_sc[j] for j in range(d // LANES)], axis=1)
    y = jax.nn.gelu(y_ssm + d_ref[...] * u).astype(BF16)
    z = _dot(y, w_ref[...]) + b_ref[...]
    o_ref[...] = x + z[:, :d] * jax.nn.sigmoid(z[:, d:])


def _s5_out(x2, y3, g, d_skip, w_glu, b_glu, tm=512):
    t, d = x2.shape
    n_j = d // LANES
    return pl.pallas_call(
        _s5_out_body,
        grid=(t // tm,),
        in_specs=[
            pl.BlockSpec((tm, d), lambda i: (i, 0)),
            pl.BlockSpec((n_j, tm // S5_L, S5_L * LANES), lambda i: (0, i, 0)),
            _const_spec((1, d)), _const_spec((1, d)),
            _const_spec((d, 2 * d)), _const_spec((1, 2 * d)),
        ],
        out_specs=pl.BlockSpec((tm, d), lambda i: (i, 0)),
        out_shape=jax.ShapeDtypeStruct((t, d), F32),
        scratch_shapes=[pltpu.VMEM((n_j, tm, LANES), F32)],
        compiler_params=_cp("parallel"),
        name="s5_out",
    )(x2, y3, _row(g), _row(d_skip), w_glu.astype(BF16), _row(b_glu))


def _s5_layer(x3, g, lam_re, lam_im, log_step, b_re, b_im, c_re, c_im, d_skip, w_glu, b_glu):
    b, s, d = x3.shape
    x2 = x3.reshape(b * s, d)
    m_op, b_op, c_op, a_pw = _s5_operators(lam_re, lam_im, log_step, b_re, b_im, c_re, c_im)
    u3 = _s5_regroup(x2, g)
    y3 = _s5_chunks(u3, m_op, b_op, c_op, a_pw, b)
    return _s5_out(x2, y3, g, d_skip, w_glu, b_glu).reshape(b, s, d)


NSA_HEAD_DIM = 64
NSA_HEADS = 16
NSA_KV_GROUPS = 4
NSA_REP = NSA_HEADS // NSA_KV_GROUPS
NSA_CMP_BLOCK = 32
NSA_CMP_STRIDE = 16
NSA_SLC_BLOCK = 64
NSA_TOP_N = 16
NSA_LOCAL_BLOCKS = 2
NSA_WINDOW = 512
ROPE_THETA = 500000.0
ROPE_DIMS = NSA_HEAD_DIM // 4
NEG_INF = -1e30
BIG = 1e9
LOG2_E = 1.4426950408889634
NSA_MAX_SEL_BLOCKS = 64
NSA_NORM_SECTIONS = 8
NSA_SEL_TILE = 256


def _rope_body(pos_ref, inv_ref, c_ref, s1_ref, s2_ref):
    ang = pos_ref[...].astype(F32) * inv_ref[...]
    lane = lax.broadcasted_iota(jnp.int32, ang.shape, 1) % NSA_HEAD_DIM
    cos, sin = jnp.cos(ang), jnp.sin(ang)
    half = ROPE_DIMS // 2
    c_ref[...] = jnp.where(lane < ROPE_DIMS, cos, 1.0)
    s1_ref[...] = jnp.where(lane < half, -sin, 0.0)
    s2_ref[...] = jnp.where(lane < half, 0.0, jnp.where(lane < ROPE_DIMS, sin, 0.0))


def _rope_tables(positions):
    s = positions.shape[0]
    half = ROPE_DIMS // 2
    inv_freq = ROPE_THETA ** (-jnp.arange(half, dtype=F32) / half)
    lane = np.arange(LANES) % NSA_HEAD_DIM
    inv_lane = jnp.where(jnp.asarray(lane < ROPE_DIMS), inv_freq[lane % half], 0.0).reshape(1, LANES)
    ts = min(s, 512)
    spec = pl.BlockSpec((ts, LANES), lambda i: (i, 0))
    return pl.pallas_call(
        _rope_body,
        grid=(s // ts,),
        in_specs=[pl.BlockSpec((ts, 1), lambda i: (i, 0)), _const_spec((1, LANES))],
        out_specs=[spec, spec, spec],
        out_shape=[jax.ShapeDtypeStruct((s, LANES), F32)] * 3,
        compiler_params=_cp("parallel"),
        name="rope_tables",
    )(positions.reshape(s, 1), inv_lane)


def _nsa_proj_body(x_ref, g_ref, wn_ref, wv_ref, wg_ref, gain_ref, bd_ref, c_ref, s1_ref, s2_ref,
                   q_ref, kc_ref, kw_ref, ksa_ref, vc_ref, vs_ref, vw_ref, gt_ref, row_sc):
    tm, d = x_ref.shape
    sw = 2 * LANES
    s_idx = pl.program_id(1)

    def store_token_groups(ref, z):
        st = NSA_CMP_STRIDE
        for lt in range(sw // LANES):
            row_sc[lt] = z[:, lt * LANES:(lt + 1) * LANES]
        for j in range(st):
            for lt in range(sw // LANES):
                ref[:, j * sw + lt * LANES:j * sw + (lt + 1) * LANES] = (
                    row_sc[lt, pl.ds(j, tm // st, stride=st), :].astype(BF16))

    h = _rms(x_ref[...], g_ref[...]).astype(BF16)
    c2 = jnp.concatenate([c_ref[...]] * 2, axis=1)
    s1 = jnp.concatenate([s1_ref[...]] * 2, axis=1)
    s2 = jnp.concatenate([s2_ref[...]] * 2, axis=1)
    blk = (s_idx * tm + lax.broadcasted_iota(jnp.int32, (tm, 1), 0)) // NSA_SLC_BLOCK
    lane = lax.broadcasted_iota(jnp.int32, (tm, sw), 1) % LANES
    blk_id = jnp.where(lane - NSA_HEAD_DIM == blk, NEG_INF, 0.0)
    inv_dh = 1.0 / NSA_HEAD_DIM
    for sec in range(NSA_NORM_SECTIONS):
        z = _dot(h, wn_ref[:, sec * sw:(sec + 1) * sw])
        aug = sec >= 6
        ss = _dot((z * z).astype(BF16), bd_ref[1 if aug else 0]) * inv_dh
        z = z * lax.rsqrt(ss + RMS_EPS) * gain_ref[:, sec * sw:(sec + 1) * sw]
        z = z * c2 + pltpu.roll(z, sw - ROPE_DIMS // 2, axis=1) * s1 + pltpu.roll(z, ROPE_DIMS // 2, axis=1) * s2
        if sec < 4:
            q_ref[:, sec * sw:(sec + 1) * sw] = z.astype(BF16)
        elif sec == 4:
            store_token_groups(kc_ref, z)
        elif sec == 5:
            for g in range(NSA_KV_GROUPS):
                kw_ref[g] = z[:, g * NSA_HEAD_DIM:(g + 1) * NSA_HEAD_DIM].astype(BF16)
        else:
            ksa_ref[:, (sec - 6) * sw:(sec - 5) * sw] = (z + blk_id).astype(BF16)
    v = _dot(h, wv_ref[...])
    store_token_groups(vc_ref, v[:, 0:sw])
    ones_col = jnp.where(lax.broadcasted_iota(jnp.int32, (tm, NSA_HEAD_DIM), 1) == 0, 1.0, 0.0)
    for g in range(NSA_KV_GROUPS):
        vs_g = v[:, sw + g * NSA_HEAD_DIM:sw + (g + 1) * NSA_HEAD_DIM]
        vw_g = v[:, 2 * sw + g * NSA_HEAD_DIM:2 * sw + (g + 1) * NSA_HEAD_DIM]
        vs_aug = jnp.concatenate([vs_g, ones_col], axis=1)
        for c in range(tm // NSA_SEL_TILE):
            vs_ref[g, c] = vs_aug[c * NSA_SEL_TILE:(c + 1) * NSA_SEL_TILE].T.astype(BF16)
        vw_ref[g] = jnp.concatenate([vw_g, ones_col], axis=1).astype(BF16)
    gt_ref[...] = jax.nn.sigmoid(_dot(h, wg_ref[...]))


def _nsa_proj(x3, g_mix, w_in, q_gain, k_gain, rope, tm=512):
    b, s, d = x3.shape
    hh, gg, dh = NSA_HEADS, NSA_KV_GROUPS, NSA_HEAD_DIM
    qd, gw = hh * dh, gg * dh
    tm = min(tm, s)
    wq = w_in[:, :qd]
    wkv = w_in[:, qd:qd + 6 * gw].reshape(d, 3, 2, gw)
    wgate = jnp.pad(w_in[:, qd + 6 * gw:], ((0, 0), (0, LANES - 3 * hh)))

    def spread(a):
        a = a.reshape(a.shape[:-1] + (gg, dh))
        return jnp.pad(a, [(0, 0)] * (a.ndim - 1) + [(0, LANES - dh)]).reshape(a.shape[:-2] + (gg * LANES,))

    w_norm = jnp.concatenate([wq, wkv[:, 0, 0], wkv[:, 2, 0], spread(wkv[:, 1, 0])], axis=1).astype(BF16)
    w_v = jnp.concatenate([wkv[:, 0, 1], wkv[:, 1, 1], wkv[:, 2, 1]], axis=1).astype(BF16)
    gain = jnp.concatenate([jnp.tile(q_gain, hh) * (dh ** -0.5 * LOG2_E), jnp.tile(k_gain[0], gg), jnp.tile(k_gain[2], gg),
                            spread(jnp.tile(k_gain[1], gg))]).reshape(1, -1).astype(F32)
    lane = np.arange(2 * LANES)
    bd = np.stack([(lane[:, None] // dh == lane[None, :] // dh), (lane[:, None] // LANES == lane[None, :] // LANES)])
    bd = jnp.asarray(bd.astype(np.float32), BF16)
    n_norm = NSA_NORM_SECTIONS * 2 * LANES
    rope_spec = pl.BlockSpec((tm, LANES), lambda i, j: (j, 0))
    tok = lambda w: pl.BlockSpec((None, tm, w), lambda i, j: (i, j, 0))
    grp = pl.BlockSpec((None, gg, tm, dh), lambda i, j: (i, 0, j, 0))
    grp_v = pl.BlockSpec((None, gg, tm, LANES), lambda i, j: (i, 0, j, 0))
    assert tm % NSA_SEL_TILE == 0
    grp_vt = pl.BlockSpec((None, gg, tm // NSA_SEL_TILE, LANES, NSA_SEL_TILE), lambda i, j: (i, 0, j, 0, 0))
    st = NSA_CMP_STRIDE
    cmp_in = pl.BlockSpec((None, tm // st, st * gw), lambda i, j: (i, j, 0))
    sds = jax.ShapeDtypeStruct
    return pl.pallas_call(
        _nsa_proj_body,
        grid=(b, s // tm),
        in_specs=[tok(d), _const_spec((1, d)), _const_spec((d, n_norm)), _const_spec((d, 3 * gw)),
                  _const_spec((d, LANES)), _const_spec((1, n_norm)), _const_spec((2, 2 * LANES, 2 * LANES)),
                  rope_spec, rope_spec, rope_spec],
        out_specs=[tok(qd), cmp_in, grp, tok(gg * LANES), cmp_in, grp_vt, grp_v, tok(LANES)],
        out_shape=[sds((b, s, qd), BF16), sds((b, s // st, st * gw), BF16), sds((b, gg, s, dh), BF16),
                   sds((b, s, gg * LANES), BF16), sds((b, s // st, st * gw), BF16),
                   sds((b, gg, s // NSA_SEL_TILE, LANES, NSA_SEL_TILE), BF16),
                   sds((b, gg, s, LANES), BF16), sds((b, s, LANES), F32)],
        scratch_shapes=[pltpu.VMEM((gw // LANES, tm, LANES), F32)],
        compiler_params=_cp("parallel", "parallel"),
        name="nsa_proj",
    )(x3, _row(g_mix), w_norm, w_v, wgate.astype(BF16), gain, bd, *rope)


def _nsa_compress_body(kin_ref, vin_ref, wa_ref, wb_ref, pa_ref, pb_ref, b1_ref, w2_ref, k_ref, vt_ref, q_sc, *, n_cmp):
    nr = kin_ref.shape[0]
    dh = NSA_HEAD_DIM
    row = lax.broadcasted_iota(jnp.int32, (nr, 1), 0)
    for c, in_ref in enumerate((kin_ref, vin_ref)):
        x = in_ref[...].astype(F32)
        first = _dot((x + pa_ref[c]).astype(BF16), wa_ref[c])
        q_sc[0:nr, :] = _dot((x + pb_ref[c]).astype(BF16), wb_ref[c])
        q_sc[nr:nr + SUBLANES, :] = jnp.zeros((SUBLANES, q_sc.shape[1]), F32)
        hid = jax.nn.gelu(first + q_sc[1:nr + 1, :] + b1_ref[c])
        comp = jnp.where(row < n_cmp, _dot(hid.astype(BF16), w2_ref[c]), 0.0)
        if c == 0:
            for g in range(NSA_KV_GROUPS):
                k_ref[g] = comp[:, g * dh:(g + 1) * dh].astype(BF16)
        else:
            comp_t = comp.T
            for g in range(NSA_KV_GROUPS):
                vt_ref[g] = comp_t[g * dh:(g + 1) * dh, :].astype(BF16)


def _nsa_compress(kc16, vc16, cmp_pos, cmp_w1, cmp_b1, cmp_w2):
    b, nr, _ = kc16.shape
    gg, dh, st = NSA_KV_GROUPS, NSA_HEAD_DIM, NSA_CMP_STRIDE
    gw = gg * dh
    s = nr * st
    n_cmp = (s - NSA_CMP_BLOCK) // st + 1
    eye = jnp.eye(gg, dtype=F32)
    w1 = cmp_w1.reshape(2, 2, st, dh, dh)
    wexp = jnp.einsum('chjde,gf->chjgdfe', w1, eye).reshape(2, 2, st * gw, gw).astype(BF16)
    pos = jnp.broadcast_to(cmp_pos.reshape(2, 2, st, 1, dh), (2, 2, st, gg, dh)).reshape(2, 2, 1, st * gw).astype(F32)
    b1 = jnp.tile(cmp_b1, (1, gg)).reshape(2, 1, gw).astype(F32)
    w2 = jnp.einsum('cde,gf->cgdfe', cmp_w2, eye).reshape(2, gw, gw).astype(BF16)
    full = lambda shape: _const_spec(shape)
    return pl.pallas_call(
        functools.partial(_nsa_compress_body, n_cmp=n_cmp),
        grid=(b,),
        in_specs=[pl.BlockSpec((None, nr, st * gw), lambda i: (i, 0, 0)),
                  pl.BlockSpec((None, nr, st * gw), lambda i: (i, 0, 0)),
                  full((2, st * gw, gw)), full((2, st * gw, gw)), full((2, 1, st * gw)), full((2, 1, st * gw)),
                  full((2, 1, gw)), full((2, gw, gw))],
        out_specs=[pl.BlockSpec((None, gg, nr, dh), lambda i: (i, 0, 0, 0)),
                   pl.BlockSpec((None, gg, dh, nr), lambda i: (i, 0, 0, 0))],
        out_shape=[jax.ShapeDtypeStruct((b, gg, nr, dh), BF16), jax.ShapeDtypeStruct((b, gg, dh, nr), BF16)],
        scratch_shapes=[pltpu.VMEM((nr + SUBLANES, gw), F32)],
        compiler_params=_cp("parallel"),
        name="nsa_compress",
    )(kc16, vc16, wexp[:, 0], wexp[:, 1], pos[:, 0], pos[:, 1], b1, w2)


def _stacked_row_index(n_rep, tq):
    assert tq & (tq - 1) == 0
    return jnp.bitwise_and(lax.broadcasted_iota(jnp.int32, (n_rep * tq, 1), 0), tq - 1)


def _nt_dot(a, b):
    return lax.dot_general(a, b, (((1,), (1,)), ((), ())), preferred_element_type=F32)


def _nsa_cmp_select_body(q_ref, kc_ref, vt_ref, ovl_ref, oc_ref, ns_ref, sc_sc, *, n_levels):
    tq = q_ref.shape[0]
    nr = kc_ref.shape[0]
    dh = NSA_HEAD_DIM
    nb = NSA_MAX_SEL_BLOCKS
    q0 = pl.program_id(2) * tq
    t_lane = q0 + lax.broadcasted_iota(jnp.int32, (1, tq), 1)
    sub = lax.broadcasted_iota(jnp.int32, (SUBLANES, 1), 0)
    span = SUBLANES * NSA_SLC_BLOCK
    level = (q0 + tq - 1) // span

    def at_level(lv):
        n_tiles = lv + 1
        nbl = n_tiles * SUBLANES
        nrl = min(nr, -(-(n_tiles * span // NSA_CMP_STRIDE) // LANES) * LANES)
        kc, vt = kc_ref[0:nrl, :], vt_ref[:, 0:nrl]
        end_row = lax.broadcasted_iota(jnp.int32, (nrl, 1), 0) * NSA_CMP_STRIDE + (NSA_CMP_BLOCK - 1)
        mask_t = end_row <= t_lane
        psum_t = jnp.zeros((nrl, tq), F32)
        outs_t = []
        for h in range(NSA_REP):
            qh = q_ref[:, h * dh:(h + 1) * dh]
            s_t = jnp.where(mask_t, _nt_dot(kc, qh), NEG_INF)
            m_t = jnp.maximum(jnp.max(s_t, axis=0, keepdims=True), 0.1 * NEG_INF)
            e_t = jnp.exp2(s_t - m_t)
            den_t = jnp.sum(e_t, axis=0, keepdims=True)
            p_t = e_t * (1.0 / jnp.where(den_t > 0.0, den_t, 1.0))
            psum_t = psum_t + p_t
            outs_t.append(_dot(vt, p_t.astype(BF16)))
        oc_ref[...] = jnp.concatenate(outs_t, axis=0).T.astype(BF16)

        p_hi = psum_t.astype(BF16)
        p_lo = (psum_t - p_hi.astype(F32)).astype(BF16)
        ovl = ovl_ref[0:-(-nbl // 16) * 16, 0:nrl]
        imp = (_dot(ovl, p_hi) + _dot(ovl, p_lo))[0:nbl]
        blk = lax.broadcasted_iota(jnp.int32, (nbl, 1), 0)
        cur = t_lane // NSA_SLC_BLOCK
        valid = blk <= cur
        forced = (blk == 0) | (blk >= cur - (NSA_LOCAL_BLOCKS - 1))
        score = jnp.where(valid, jnp.where(forced, BIG, imp), -BIG)
        sc_sc[0:nbl, :] = score
        tiles = [score[a * SUBLANES:(a + 1) * SUBLANES] for a in range(n_tiles)]
        ranks = [jnp.zeros((SUBLANES, tq), F32) for _ in range(n_tiles)]
        for j in range(nbl):
            sj = sc_sc[j:j + 1, :]
            jt = j // SUBLANES
            for a in range(n_tiles):
                if a > jt:
                    ahead = jnp.where(sj >= tiles[a], 1.0, 0.0)
                elif a < jt:
                    ahead = jnp.where(sj > tiles[a], 1.0, 0.0)
                else:
                    ahead = jnp.where(sub > j - a * SUBLANES,
                                      jnp.where(sj >= tiles[a], 1.0, 0.0), jnp.where(sj > tiles[a], 1.0, 0.0))
                ranks[a] = ranks[a] + ahead
        notsel = [jnp.where(r < float(NSA_TOP_N), 0.0, 1.0) for r in ranks]
        notsel = jnp.concatenate(notsel + [jnp.ones((LANES - nbl, tq), F32)], axis=0)
        ns_ref[...] = notsel.T[:, :nb].astype(BF16)

    for lv in range(n_levels):
        pl.when(level == lv)(functools.partial(at_level, lv))


def _nsa_cmp_select(q, k_cmp, vt_cmp, tq=512):
    b, s, qd = q.shape
    gg, dh = NSA_KV_GROUPS, NSA_HEAD_DIM
    nr = k_cmp.shape[2]
    nb = NSA_MAX_SEL_BLOCKS
    tq = min(tq, s)
    assert s // NSA_SLC_BLOCK <= nb
    c_start = np.arange(nr) * NSA_CMP_STRIDE
    c_end = c_start + NSA_CMP_BLOCK - 1
    s_start = np.arange(nb) * NSA_SLC_BLOCK
    s_end = s_start + NSA_SLC_BLOCK - 1
    ovl = ((c_start[None, :] <= s_end[:, None]) & (c_end[None, :] >= s_start[:, None])).astype(np.float32)
    n_levels = -(-s // (SUBLANES * NSA_SLC_BLOCK))
    return pl.pallas_call(
        functools.partial(_nsa_cmp_select_body, n_levels=n_levels),
        grid=(b, gg, s // tq),
        in_specs=[pl.BlockSpec((None, tq, NSA_REP * dh), lambda i, g, j: (i, j, g)),
                  pl.BlockSpec((None, None, nr, dh), lambda i, g, j: (i, g, 0, 0)),
                  pl.BlockSpec((None, None, dh, nr), lambda i, g, j: (i, g, 0, 0)),
                  _const_spec((nb, nr))],
        out_specs=[pl.BlockSpec((None, tq, NSA_REP * dh), lambda i, g, j: (i, j, g)),
                   pl.BlockSpec((None, None, tq, nb), lambda i, g, j: (i, g, j, 0))],
        out_shape=[jax.ShapeDtypeStruct((b, s, qd), BF16), jax.ShapeDtypeStruct((b, gg, s, nb), BF16)],
        scratch_shapes=[pltpu.VMEM((nb, tq), F32)],
        compiler_params=_cp("parallel", "parallel", "parallel"),
        name="nsa_cmp_select",
    )(q, k_cmp, vt_cmp, jnp.asarray(ovl, BF16))


def _nsa_sel_body(q_ref, ns_ref, k_ref, vt_ref, o_ref, qa_sc, s0_sc, s1_sc, m_sc, acc_sc):
    tq = q_ref.shape[0]
    dh = NSA_HEAD_DIM
    qi = pl.program_id(2)
    for h in range(NSA_REP):
        qa_sc[h * tq:(h + 1) * tq, :] = jnp.concatenate([q_ref[:, h * dh:(h + 1) * dh], ns_ref[...]], axis=1)
    m_sc[...] = jnp.full(m_sc.shape, NEG_INF, F32)
    acc_sc[...] = jnp.zeros(acc_sc.shape, F32)
    k_pos = lax.broadcasted_iota(jnp.int32, (tq, 1), 0)
    t_pos = lax.broadcasted_iota(jnp.int32, (1, tq), 1)

    def scores(kt, s_sc):
        start = pl.multiple_of(kt * tq, tq)
        s_sc[...] = _nt_dot(k_ref[pl.ds(start, tq), :], qa_sc[...])

    def attend(kt, s_sc, diagonal):
        vt = vt_ref[kt]
        for h in range(NSA_REP):
            cols = slice(h * tq, (h + 1) * tq)
            sh = s_sc[:, cols]
            if diagonal:
                sh = jnp.where(k_pos <= t_pos, sh, NEG_INF)
            m_old = m_sc[:, cols]
            m_new = jnp.maximum(m_old, jnp.max(sh, axis=0, keepdims=True))
            p = jnp.exp2(sh - m_new)
            acc_sc[:, cols] = jnp.exp2(m_old - m_new) * acc_sc[:, cols] + _dot(vt, p.astype(BF16))
            m_sc[:, cols] = m_new

    def tile_pair(i, carry):
        scores(2 * i + 1, s1_sc)
        attend(2 * i, s0_sc, False)
        scores(2 * i + 2, s0_sc)
        attend(2 * i + 1, s1_sc, False)
        return carry

    scores(0, s0_sc)
    lax.fori_loop(0, qi // 2, tile_pair, 0)

    @pl.when(qi % 2 == 0)
    def _():
        attend(qi, s0_sc, True)

    @pl.when(qi % 2 == 1)
    def _():
        scores(qi, s1_sc)
        attend(qi - 1, s0_sc, False)
        attend(qi, s1_sc, True)

    outs = []
    for h in range(NSA_REP):
        acc = acc_sc[:, h * tq:(h + 1) * tq].T
        outs.append(acc[:, :dh] / acc[:, dh:dh + 1])
    o_ref[...] = jnp.concatenate(outs, axis=1).astype(BF16)


def _nsa_selected(q, notsel, ksa, vst):
    b, s, qd = q.shape
    gg, dh, rr = NSA_KV_GROUPS, NSA_HEAD_DIM, NSA_REP
    tq = NSA_SEL_TILE
    assert s % tq == 0 and tq % LANES == 0
    return pl.pallas_call(
        _nsa_sel_body,
        grid=(b, gg, s // tq),
        in_specs=[pl.BlockSpec((None, tq, rr * dh), lambda i, g, j: (i, j, g)),
                  pl.BlockSpec((None, None, tq, NSA_MAX_SEL_BLOCKS), lambda i, g, j: (i, g, j, 0)),
                  pl.BlockSpec((None, s, LANES), lambda i, g, j: (i, 0, g)),
                  pl.BlockSpec((None, None, s // tq, LANES, tq), lambda i, g, j: (i, g, 0, 0, 0))],
        out_specs=pl.BlockSpec((None, tq, rr * dh), lambda i, g, j: (i, j, g)),
        out_shape=jax.ShapeDtypeStruct((b, s, qd), BF16),
        scratch_shapes=[pltpu.VMEM((rr * tq, LANES), BF16), pltpu.VMEM((tq, rr * tq), F32),
                        pltpu.VMEM((tq, rr * tq), F32), pltpu.VMEM((1, rr * tq), F32),
                        pltpu.VMEM((LANES, rr * tq), F32)],
        compiler_params=_cp("parallel", "parallel", "arbitrary"),
        name="nsa_selected",
    )(q, notsel, ksa, vst)


def _nsa_window_body(q_ref, *refs, n_kv):
    k_refs, v_refs, o_ref = refs[:n_kv], refs[n_kv:2 * n_kv], refs[2 * n_kv]
    tq = q_ref.shape[0]
    dh = NSA_HEAD_DIM
    qi = pl.program_id(2)
    qs = jnp.concatenate([q_ref[:, h * dh:(h + 1) * dh] for h in range(NSA_REP)], axis=0)
    t_pos = lax.broadcasted_iota(jnp.int32, (tq, 1), 0)
    k_rel = lax.broadcasted_iota(jnp.int32, (1, tq), 1)
    scores = [_nt_dot(qs, k_refs[j][...]) for j in range(n_kv)]
    outs = []
    for h in range(NSA_REP):
        parts = []
        for j in range(n_kv):
            sh = scores[j][h * tq:(h + 1) * tq]
            back = (n_kv - 1 - j) * tq
            in_seq = qi >= n_kv - 1 - j
            if j == n_kv - 1:
                parts.append(jnp.where(k_rel <= t_pos, sh, NEG_INF))
            elif j == 0:
                limit = jnp.where(in_seq, NSA_WINDOW, -(2 ** 30))
                parts.append(jnp.where(t_pos - k_rel + back < limit, sh, NEG_INF))
            else:
                parts.append(sh + jnp.where(in_seq, 0.0, NEG_INF))
        m = functools.reduce(jnp.maximum, [jnp.max(p_, axis=-1, keepdims=True) for p_ in parts])
        parts = [jnp.exp2(p_ - m) for p_ in parts]
        o = functools.reduce(jnp.add, [_dot(parts[j].astype(BF16), v_refs[j][...]) for j in range(n_kv)])
        outs.append(o[:, :dh] / o[:, dh:dh + 1])
    o_ref[...] = jnp.concatenate(outs, axis=1).astype(BF16)


def _nsa_window(q, kw, vw, tq=256):
    b, s, qd = q.shape
    gg, dh, rr = NSA_KV_GROUPS, NSA_HEAD_DIM, NSA_REP
    tq = min(tq, s)
    assert NSA_WINDOW % tq == 0
    n_kv = NSA_WINDOW // tq + 1
    kv_spec = lambda off, w: pl.BlockSpec((None, None, tq, w), lambda i, g, j: (i, g, jnp.maximum(j - off, 0), 0))
    k_specs = [kv_spec(n_kv - 1 - j, dh) for j in range(n_kv)]
    v_specs = [kv_spec(n_kv - 1 - j, LANES) for j in range(n_kv)]
    return pl.pallas_call(
        functools.partial(_nsa_window_body, n_kv=n_kv),
        grid=(b, gg, s // tq),
        in_specs=[pl.BlockSpec((None, tq, rr * dh), lambda i, g, j: (i, j, g))] + k_specs + v_specs,
        out_specs=pl.BlockSpec((None, tq, rr * dh), lambda i, g, j: (i, j, g)),
        out_shape=jax.ShapeDtypeStruct((b, s, qd), BF16),
        compiler_params=_cp("parallel", "parallel", "parallel"),
        name="nsa_window",
    )(q, *([kw] * n_kv), *([vw] * n_kv))


def _nsa_out_body(x_ref, oc_ref, os_ref, ow_ref, gt_ref, ex_ref, w_ref, o_ref):
    qd = oc_ref.shape[1]
    gt = gt_ref[...]
    g_hi = gt.astype(BF16)
    g_lo = (gt - g_hi.astype(F32)).astype(BF16)
    o = jnp.zeros(oc_ref.shape, F32)
    for br, ref in enumerate((oc_ref, os_ref, ow_ref)):
        ex = ex_ref[:, br * qd:(br + 1) * qd]
        o = o + (_dot(g_hi, ex) + _dot(g_lo, ex)) * ref[...].astype(F32)
    o_ref[...] = x_ref[...] + _dot(o.astype(BF16), w_ref[...])


def _nsa_out(x2, oc, osel, ow, gates, w_out, tm=512):
    t, d = x2.shape
    qd = oc.shape[1]
    tm = min(tm, t)
    col = np.arange(3 * qd)
    expand = (np.arange(LANES)[:, None] == (col // qd) * NSA_HEADS + (col % qd) // NSA_HEAD_DIM).astype(np.float32)
    tok = lambda w: pl.BlockSpec((tm, w), lambda i: (i, 0))
    return pl.pallas_call(
        _nsa_out_body,
        grid=(t // tm,),
        in_specs=[tok(d), tok(qd), tok(qd), tok(qd), tok(LANES), _const_spec((LANES, 3 * qd)), _const_spec((qd, d))],
        out_specs=tok(d),
        out_shape=jax.ShapeDtypeStruct((t, d), F32),
        compiler_params=_cp("parallel"),
        name="nsa_out",
    )(x2, oc, osel, ow, gates, jnp.asarray(expand, BF16), w_out.astype(BF16))


def _nsa_layer(x3, positions, g_mix, w_in, q_gain, k_gain, cmp_pos, cmp_w1, cmp_b1, cmp_w2, w_out):
    b, s, d = x3.shape
    rope = _rope_tables(positions)
    q, kc, kw, ksa, vc, vs, vw, gates = _nsa_proj(x3, g_mix, w_in, q_gain, k_gain, rope)
    k_cmp, vt_cmp = _nsa_compress(kc, vc, cmp_pos, cmp_w1, cmp_b1, cmp_w2)
    oc, notsel = _nsa_cmp_select(q, k_cmp, vt_cmp)
    osel = _nsa_selected(q, notsel, ksa, vs)
    ow = _nsa_window(q, kw, vw)
    t = b * s
    flat = lambda a: a.reshape(t, a.shape[-1])
    return _nsa_out(x3.reshape(t, d), flat(oc), flat(osel), flat(ow), flat(gates), w_out).reshape(b, s, d)


def kernel(x, positions, norm_mix, norm_ffn, ffn_w_in, ffn_w_out, conv_w_in, conv_b_in, conv_w_dw, conv_b_dw, conv_ln_g, conv_ln_b, conv_w_out, nsa_w_in, nsa_q_gain, nsa_k_gain, nsa_cmp_pos, nsa_cmp_w1, nsa_cmp_b1, nsa_cmp_w2, nsa_w_out, s5_lam_re, s5_lam_im, s5_log_step, s5_b_re, s5_b_im, s5_c_re, s5_c_im, s5_d, s5_w_glu, s5_b_glu, pool_w, pool_scale):
    b, s, d = x.shape
    depth = norm_mix.shape[0]
    for i in range(depth):
        m, j = i % N_MIXERS, i // N_MIXERS
        if m == 0:
            x = _conv_layer(x, norm_mix[i], conv_w_in[j], conv_b_in[j], conv_w_dw[j], conv_b_dw[j],
                            conv_ln_g[j], conv_ln_b[j], conv_w_out[j])
        elif m == 1:
            x = _nsa_layer(x, positions, norm_mix[i], nsa_w_in[j], nsa_q_gain[j], nsa_k_gain[j], nsa_cmp_pos[j],
                           nsa_cmp_w1[j], nsa_cmp_b1[j], nsa_cmp_w2[j], nsa_w_out[j])
        elif m == 2:
            x = _s5_layer(x, norm_mix[i], s5_lam_re[j], s5_lam_im[j], s5_log_step[j], s5_b_re[j], s5_b_im[j],
                          s5_c_re[j], s5_c_im[j], s5_d[j], s5_w_glu[j], s5_b_glu[j])
        elif m == 3:
            x = _pool_layer(x, norm_mix[i], pool_w[j], pool_scale[j])
        x = _ffn(x.reshape(b * s, d), norm_ffn[i], ffn_w_in[i], ffn_w_out[i]).reshape(b, s, d)
    return x
```
